```python
import math
import jax, jax.numpy as jnp
from jax import lax
import numpy as np

D_MODEL = 1024
BATCH = 16
SEQ = 2048
DEPTH = 2

CTX_LEN = 256
GRID_W = 64
CHUNK = 128
ROWS_PER_CHUNK = CHUNK // GRID_W
EPS = 1e-6

SSD_HEAD_DIM = 64
SSD_HEADS = D_MODEL // SSD_HEAD_DIM
SSD_WIDTH = SSD_HEADS * SSD_HEAD_DIM
SSD_GROUPS = 4
SSD_STATE = 128
SSD_CONV = 3
DT_MIN = 0.001
DT_MAX = 0.1
GN = SSD_GROUPS * SSD_STATE
N_XBC = SSD_WIDTH + 2 * GN
N_SCAN = N_XBC + 2 * SSD_HEADS

MLP_WIDTH = D_MODEL
MLP_GROUP_DIM = 128
MLP_GROUPS = MLP_WIDTH // MLP_GROUP_DIM

N_IN = N_SCAN + SSD_WIDTH + 2 * MLP_WIDTH + 2 * D_MODEL

N_EXPERTS = 16
EXPERT_FF = D_MODEL
CAPACITY_FACTOR = 2

kernel_name = 'hybrid_ssd_gmlp_ecmoe_dit_prefix'


def split_cols(p, sizes):
    idx = np.cumsum(sizes)[:-1].tolist()
    return jnp.split(p, idx, axis=-1)


def rmsnorm(x, w):
    xf = x.astype(jnp.float32)
    y = xf * lax.rsqrt(jnp.mean(xf * xf, axis=-1, keepdims=True) + EPS)
    return (y * w.astype(jnp.float32)).astype(x.dtype)


def layernorm(x, w, b):
    xf = x.astype(jnp.float32)
    mu = jnp.mean(xf, axis=-1, keepdims=True)
    var = jnp.mean(jnp.square(xf - mu), axis=-1, keepdims=True)
    y = (xf - mu) * lax.rsqrt(var + EPS)
    return (y * w.astype(jnp.float32) + b.astype(jnp.float32)).astype(x.dtype)


def dwconv_centred(x, w, b):
    ch = x.shape[-1]
    k = w.shape[0]
    y = lax.conv_general_dilated(x, w[:, None, :].astype(x.dtype), window_strides=(1,),
                                 padding=[(k // 2, k // 2)],
                                 dimension_numbers=('NWC', 'WIO', 'NWC'),
                                 feature_group_count=ch)
    return y + b.astype(x.dtype)


def ssd_chunked(xh, dt, A, bm, cm, h0):
    b, l, H, P = xh.shape
    G, N = bm.shape[2], bm.shape[3]
    R = H // G
    nc = l // CHUNK
    x = xh.reshape(b, nc, CHUNK, G, R, P)
    dt_c = dt.reshape(b, nc, CHUNK, G, R)
    a_cum = jnp.cumsum(dt_c * A.reshape(G, R), axis=2)
    Bc = bm.reshape(b, nc, CHUNK, G, N)
    Cc = cm.reshape(b, nc, CHUNK, G, N)
    xdt = x * dt_c[..., None].astype(x.dtype)
    lower = jnp.tril(jnp.ones((CHUNK, CHUNK), dtype=bool))[None, None, :, :, None, None]
    seg = a_cum[:, :, :, None] - a_cum[:, :, None, :]
    decay = jnp.exp(jnp.where(lower, seg, -jnp.inf))
    cb = jnp.einsum('bcign,bcjgn->bcijg', Cc, Bc)
    y_diag = jnp.einsum('bcijg,bcijgr,bcjgrp->bcigrp', cb, decay, xdt)
    decay_to_end = jnp.exp(a_cum[:, :, -1:] - a_cum)
    states = jnp.einsum('bcjgn,bcjgr,bcjgrp->bcgrpn', Bc, decay_to_end, xdt)
    chunk_decay = jnp.exp(a_cum[:, :, -1])

    def step(h, inp):
        st, dcy = inp
        return h * dcy[..., None, None] + st, h

    h_final, h_in = lax.scan(step, h0, (jnp.moveaxis(states, 1, 0).astype(jnp.float32),
                                        jnp.moveaxis(chunk_decay, 1, 0)))
    y_off = jnp.einsum('bcign,bcigr,cbgrpn->bcigrp', Cc, jnp.exp(a_cum), h_in)
    y = (y_diag + y_off).reshape(b, l, H, P).astype(xh.dtype)
    return y, h_final


def ssd_scan_bidir(p_scan, conv_w, conv_b, dt_bias, a_log, h0_f, h0_b):
    xbc, dt_raw = p_scan[..., :N_XBC], p_scan[..., N_XBC:]
    xbc = jax.nn.silu(dwconv_centred(xbc, conv_w, conv_b))
    xs, bm, cm = split_cols(xbc, [SSD_WIDTH, GN, GN])
    b, l, _ = xs.shape
    xh = xs.reshape(b, l, SSD_HEADS, SSD_HEAD_DIM)
    bm = bm.reshape(b, l, SSD_GROUPS, SSD_STATE)
    cm = cm.reshape(b, l, SSD_GROUPS, SSD_STATE)
    dt = jax.nn.softplus(dt_raw.astype(jnp.float32).reshape(b, l, 2, SSD_HEADS)
                         + dt_bias.astype(jnp.float32))
    A = -jnp.exp(a_log.astype(jnp.float32))
    y_f, st_f = ssd_chunked(xh, dt[:, :, 0], A[0], bm, cm, h0_f)
    rev = lambda t: jnp.flip(t, axis=1)
    y_b, st_b = ssd_chunked(rev(xh), rev(dt[:, :, 1]), A[1], rev(bm), rev(cm), h0_b)
    return y_f + rev(y_b), xh, st_f, st_b


def ssd_output(y, xh, z, d_skip, norm_w):
    b, l, H, P = xh.shape
    y = y + d_skip.astype(xh.dtype)[:, None] * xh
    g = y.reshape(b, l, SSD_WIDTH) * jax.nn.silu(z)
    g = rmsnorm(g.reshape(b, l, SSD_GROUPS, SSD_WIDTH // SSD_GROUPS),
                norm_w.reshape(SSD_GROUPS, SSD_WIDTH // SSD_GROUPS))
    return g.reshape(b, l, SSD_WIDTH)


def spatial_gating(u, v, n_chunks, ln_w, ln_b, w_s, b_s):
    b, l, _ = v.shape
    v = layernorm(v, ln_w, ln_b)
    vc = v.reshape(b, n_chunks, CHUNK, MLP_GROUPS, MLP_GROUP_DIM)
    mixed = jnp.einsum('gij,bcjgd->bcigd', w_s, vc) + b_s.T[None, None, :, :, None]
    return u * mixed.reshape(b, l, MLP_WIDTH)


def mixer_out(y, xh, p_rest, n_chunks, d_skip, ssd_norm_w, ln_w, ln_b, w_s, b_s, w_pa, w_pb, w_o):
    z, uv, gate_a, gate_b = split_cols(p_rest, [SSD_WIDTH, 2 * MLP_WIDTH, D_MODEL, D_MODEL])
    ssd = ssd_output(y, xh, z, d_skip, ssd_norm_w)
    u, v = jnp.split(jax.nn.gelu(uv), 2, axis=-1)
    sgu = spatial_gating(u, v, n_chunks, ln_w, ln_b, w_s, b_s)
    merged = jax.nn.sigmoid(gate_a) * (ssd @ w_pa) + jax.nn.sigmoid(gate_b) * (sgu @ w_pb)
    return merged @ w_o


def expert_choice_ffn(xn, router_w, w_gate, w_up, w_down):
    b, n, d = xn.shape
    cap = CAPACITY_FACTOR * n // N_EXPERTS
    logits = jnp.einsum('bnd,de->ben', xn, router_w).astype(jnp.float32)
    aff = jax.nn.softmax(logits, axis=1)
    gate, idx = lax.top_k(aff, cap)
    xg = jax.vmap(lambda xs, ix: xs[ix])(xn, idx)
    hid = jax.nn.silu(jnp.einsum('becd,edf->becf', xg, w_gate)) * jnp.einsum('becd,edf->becf', xg, w_up)
    y = jnp.einsum('becf,efd->becd', hid, w_down) * gate[..., None].astype(xn.dtype)
    scatter = lambda ix, ys: jnp.zeros((n, d), ys.dtype).at[ix.reshape(-1)].add(ys.reshape(-1, d))
    return jax.vmap(scatter)(idx, y)


def setup_inputs(seed: int = 0) -> dict:
    key = jax.random.key(seed)
    ks = jax.random.split(key, 32)
    f32 = jnp.float32
    L = DEPTH
    nrm = lambda k, shape, s: jax.random.normal(k, shape, f32) * s
    gain = lambda k, shape: 1.0 + 0.01 * jax.random.normal(k, shape, f32)
    dt0 = jnp.exp(jax.random.uniform(ks[12], (L, 2, SSD_HEADS), f32, math.log(DT_MIN), math.log(DT_MAX)))
    return {
        'x': nrm(ks[0], (BATCH, SEQ, D_MODEL), 1.0),
        'c': nrm(ks[1], (BATCH, D_MODEL), 1.0),
        'ctx': nrm(ks[2], (BATCH, CTX_LEN, D_MODEL), 1.0),
        'c_ctx': nrm(ks[3], (D_MODEL,), 1.0),
        'ada_w': nrm(ks[4], (L, D_MODEL, 6 * D_MODEL), 0.5 * D_MODEL ** -0.5),
        'ada_b': nrm(ks[5], (L, 6 * D_MODEL), 0.01),
        'norm1_w': gain(ks[6], (L, D_MODEL)),
        'norm2_w': gain(ks[7], (L, D_MODEL)),
        'w_in': nrm(ks[8], (L, D_MODEL, N_IN), D_MODEL ** -0.5),
        'conv_w': nrm(ks[9], (L, SSD_CONV, N_XBC), SSD_CONV ** -0.5),
        'conv_b': nrm(ks[10], (L, N_XBC), 0.01),
        'dt_bias': dt0 + jnp.log(-jnp.expm1(-dt0)),
        'a_log': jnp.log(jax.random.uniform(ks[13], (L, 2, SSD_HEADS), f32, 1.0, 16.0)),
        'd_skip': gain(ks[14], (L, SSD_HEADS)),
        'ssd_norm_w': gain(ks[15], (L, SSD_WIDTH)),
        'sgu_ln_w': gain(ks[16], (L, MLP_WIDTH)),
        'sgu_ln_b': nrm(ks[17], (L, MLP_WIDTH), 0.01),
        'w_s': nrm(ks[18], (L, MLP_GROUPS, CHUNK, CHUNK), CHUNK ** -0.5),
        'b_s': gain(ks[19], (L, MLP_GROUPS, CHUNK)),
        'w_pa': nrm(ks[20], (L, SSD_WIDTH, D_MODEL), SSD_WIDTH ** -0.5),
        'w_pb': nrm(ks[21], (L, MLP_WIDTH, D_MODEL), MLP_WIDTH ** -0.5),
        'w_o': nrm(ks[22], (L, D_MODEL, D_MODEL), D_MODEL ** -0.5),
        'router_w': nrm(ks[23], (L, D_MODEL, N_EXPERTS), D_MODEL ** -0.5),
        'w_gate': nrm(ks[24], (L, N_EXPERTS, D_MODEL, EXPERT_FF), D_MODEL ** -0.5),
        'w_up': nrm(ks[25], (L, N_EXPERTS, D_MODEL, EXPERT_FF), D_MODEL ** -0.5),
        'w_down': nrm(ks[26], (L, N_EXPERTS, EXPERT_FF, D_MODEL), EXPERT_FF ** -0.5),
        'final_norm_w': gain(ks[27], (D_MODEL,)),
    }


def reference(x, c, ctx, c_ctx, ada_w, ada_b, norm1_w, norm2_w, w_in, conv_w, conv_b, dt_bias, a_log,
              d_skip, ssd_norm_w, sgu_ln_w, sgu_ln_b, w_s, b_s, w_pa, w_pb, w_o, router_w,
              w_gate, w_up, w_down, final_norm_w):
    bsz, n_lat, _ = x.shape
    rows = n_lat // GRID_W
    lat_chunks = rows // ROWS_PER_CHUNK
    ctx_chunks = ctx.shape[1] // CHUNK
    h0 = jnp.zeros((bsz, SSD_GROUPS, SSD_HEADS // SSD_GROUPS, SSD_HEAD_DIM, SSD_STATE), jnp.float32)
    h, hc = x, ctx
    for layer in range(DEPTH):
        last = layer == DEPTH - 1
        mod_l = (jax.nn.silu(c) @ ada_w[layer] + ada_b[layer])[:, None, :]
        mod_c = (jax.nn.silu(c_ctx) @ ada_w[layer] + ada_b[layer])[None, None, :]
        sh1_l, sc1_l, g1_l, sh2_l, sc2_l, g2_l = jnp.split(mod_l, 6, axis=-1)
        sh1_c, sc1_c, g1_c, sh2_c, sc2_c, g2_c = jnp.split(mod_c, 6, axis=-1)
        w_in_l = w_in[layer]
        mix_args = (d_skip[layer], ssd_norm_w[layer], sgu_ln_w[layer], sgu_ln_b[layer],
                    w_s[layer], b_s[layer], w_pa[layer], w_pb[layer], w_o[layer])
        xm_c = rmsnorm(hc, norm1_w[layer]) * (1 + sc1_c) + sh1_c
        xm_l = rmsnorm(h, norm1_w[layer]) * (1 + sc1_l) + sh1_l
        y_c, xh_c, st_f, st_b = ssd_scan_bidir(xm_c @ w_in_l[:, :N_SCAN], conv_w[layer], conv_b[layer],
                                               dt_bias[layer], a_log[layer], h0, h0)
        p_l = xm_l @ w_in_l
        y_l, xh_l, _, _ = ssd_scan_bidir(p_l[..., :N_SCAN], conv_w[layer], conv_b[layer],
                                         dt_bias[layer], a_log[layer], st_f, st_b)
        h = h + g1_l * mixer_out(y_l, xh_l, p_l[..., N_SCAN:], lat_chunks, *mix_args)
        if not last:
            hc = hc + g1_c * mixer_out(y_c, xh_c, xm_c @ w_in_l[:, N_SCAN:], ctx_chunks, *mix_args)
        moe_args = (router_w[layer], w_gate[layer], w_up[layer], w_down[layer])
        xf_l = rmsnorm(h, norm2_w[layer]) * (1 + sc2_l) + sh2_l
        h = h + g2_l * expert_choice_ffn(xf_l, *moe_args)
        if not last:
            xf_c = rmsnorm(hc, norm2_w[layer]) * (1 + sc2_c) + sh2_c
            hc = hc + g2_c * expert_choice_ffn(xf_c, *moe_args)
    return rmsnorm(h, final_norm_w)
```

```python
import functools

import jax
import jax.numpy as jnp
from jax import lax
from jax.experimental import pallas as pl
from jax.experimental.pallas import tpu as pltpu

F32 = jnp.float32
BF16 = jnp.bfloat16
HIGHEST = lax.Precision.HIGHEST

D = 1024
CHUNK = 128
EPS = 1e-6
HEADS = 16
HEAD_DIM = 64
GROUPS = 4
STATE = 128
N_XBC = 2048
N_DT = 2 * HEADS
N_SCAN = N_XBC + N_DT
N_MAIN = 7 * D
MLP_GROUPS = 8
N_EXPERTS = 16
CAPACITY_FACTOR = 2
LANES = 128
VMEM_LIMIT = 56 * 1024 * 1024


def _cparams(*sem):
    return pltpu.CompilerParams(dimension_semantics=sem, vmem_limit_bytes=VMEM_LIMIT)


def _softplus(x):
    return jnp.maximum(x, 0.0) + jnp.log1p(jnp.exp(-jnp.abs(x)))


def _rms(x, w):
    return x * lax.rsqrt(jnp.mean(x * x, axis=-1, keepdims=True) + EPS) * w


def _ada_kernel(c_ref, w_ref, b_ref, o_ref):
    a = jax.nn.silu(c_ref[...]).astype(BF16)
    o_ref[0] = jnp.dot(a, w_ref[0].astype(BF16), preferred_element_type=F32) + b_ref[0]


def _ada(cc, ada_w, ada_b):
    n_layers = ada_w.shape[0]
    rows = cc.shape[0]
    return pl.pallas_call(
        _ada_kernel,
        grid=(n_layers, 6),
        in_specs=[pl.BlockSpec((rows, D), lambda l, j: (0, 0)),
                  pl.BlockSpec((1, D, D), lambda l, j: (l, 0, j)),
                  pl.BlockSpec((1, 1, D), lambda l, j: (l, 0, j))],
        out_specs=pl.BlockSpec((1, rows, D), lambda l, j: (l, 0, j)),
        out_shape=jax.ShapeDtypeStruct((n_layers, rows, 6 * D), F32),
        compiler_params=_cparams("arbitrary", "arbitrary"),
        name="ada",
    )(cc, ada_w, ada_b.reshape(n_layers, 1, 6 * D))


def _in_proj_kernel(with_res, *refs):
    if with_res:
        (h_ref, moe_ref, g_ref, sc_ref, sh_ref, nw_ref, w_ref, wdt_ref,
         p_ref, dt_ref, hout_ref, xm_scr) = refs
    else:
        h_ref, sc_ref, sh_ref, nw_ref, w_ref, wdt_ref, p_ref, dt_ref, xm_scr = refs

    @pl.when(pl.program_id(1) == 0)
    def _():
        h = h_ref[...]
        if with_res:
            h = h + g_ref[0] * moe_ref[...]
            hout_ref[...] = h
        xm = _rms(h, nw_ref[...]) * (1.0 + sc_ref[0]) + sh_ref[0]
        xmb = xm.astype(BF16)
        xm_scr[...] = xmb
        dt_ref[...] = jnp.dot(xmb, wdt_ref[...], preferred_element_type=F32)

    p_ref[...] = jnp.dot(xm_scr[...], w_ref[...], preferred_element_type=F32).astype(BF16)


def _in_proj(h2, sc, sh, norm_w, w_main, w_dt, tokens_per_mod, res=None):
    m = h2.shape[0]
    ncols = w_main.shape[1]
    tm = min(1024, tokens_per_mod)
    tiles_per_mod = tokens_per_mod // tm
    tn = D
    with_res = res is not None
    row = pl.BlockSpec((tm, D), lambda i, j: (i, 0))
    mod = pl.BlockSpec((1, 1, D), lambda i, j: (i // tiles_per_mod, 0, 0))
    in_specs = [row]
    args = [h2]
    if with_res:
        in_specs += [row, mod]
        args += [res[0], res[1]]
    in_specs += [mod, mod,
                 pl.BlockSpec((1, D), lambda i, j: (0, 0)),
                 pl.BlockSpec((D, tn), lambda i, j: (0, j)),
                 pl.BlockSpec((D, LANES), lambda i, j: (0, 0))]
    args += [sc, sh, norm_w.reshape(1, D), w_main, w_dt]
    out_specs = [pl.BlockSpec((tm, tn), lambda i, j: (i, j)),
                 pl.BlockSpec((tm, LANES), lambda i, j: (i, 0))]
    out_shape = [jax.ShapeDtypeStruct((m, ncols), BF16),
                 jax.ShapeDtypeStruct((m, LANES), F32)]
    if with_res:
        out_specs.append(row)
        out_shape.append(jax.ShapeDtypeStruct((m, D), F32))
    return pl.pallas_call(
        functools.partial(_in_proj_kernel, with_res),
        grid=(m // tm, ncols // tn),
        in_specs=in_specs, out_specs=out_specs, out_shape=out_shape,
        scratch_shapes=[pltpu.VMEM((tm, D), BF16)],
        compiler_params=_cparams("arbitrary", "arbitrary"),
        name="in_proj_res" if with_res else "in_proj",
    )(*args)


def _conv_kernel(p_ref, w_ref, b_ref, o_ref):
    x = p_ref[0].astype(F32)
    n = x.shape[0]
    row = lax.broadcasted_iota(jnp.int32, x.shape, 0)
    prev = jnp.where(row == 0, 0.0, pltpu.roll(x, 1, 0))
    nxt = jnp.where(row == n - 1, 0.0, pltpu.roll(x, n - 1, 0))
    y = prev * w_ref[0:1, :] + x * w_ref[1:2, :] + nxt * w_ref[2:3, :] + b_ref[...]
    o_ref[0] = jax.nn.silu(y).astype(o_ref.dtype)


def _conv(p3, conv_w, conv_b):
    b, l, _ = p3.shape
    tc = 256
    return pl.pallas_call(
        _conv_kernel,
        grid=(b, N_XBC // tc),
        in_specs=[pl.BlockSpec((1, l, tc), lambda i, j: (i, 0, j)),
                  pl.BlockSpec((3, tc), lambda i, j: (0, j)),
                  pl.BlockSpec((1, tc), lambda i, j: (0, j))],
        out_specs=pl.BlockSpec((1, l, tc), lambda i, j: (i, 0, j)),
        out_shape=jax.ShapeDtypeStruct((b, l, N_XBC), BF16),
        compiler_params=_cparams("arbitrary", "arbitrary"),
        name="conv",
    )(p3, conv_w, conv_b.reshape(1, N_XBC))


def _ssd_kernel(xf_ref, bf_ref, cf_ref, dtf_ref, xb_ref, bb_ref, cb_ref, dtb_ref,
                dtbias_ref, alog_ref, h0f_ref, h0b_ref,
                yf_ref, yb_ref, stf_ref, stb_ref):
    @pl.when(pl.program_id(1) == 0)
    def _():
        stf_ref[...] = h0f_ref[...]
        stb_ref[...] = h0b_ref[...]

    ri = lax.broadcasted_iota(jnp.int32, (CHUNK, CHUNK), 0)
    ci = lax.broadcasted_iota(jnp.int32, (CHUNK, CHUNK), 1)
    lane_lo = ci < HEAD_DIM
    neg_a = -jnp.exp(alog_ref[...])
    streams = ((xf_ref, bf_ref, cf_ref, dtf_ref, yf_ref, stf_ref),
               (xb_ref, bb_ref, cb_ref, dtb_ref, yb_ref, stb_ref))
    for d, (x_ref, b_ref, c_ref, dt_ref, y_ref, st_ref) in enumerate(streams):
        mask = (ci <= ri) if d == 0 else (ci >= ri)
        dtv = _softplus(dt_ref[0] + dtbias_ref[...])
        a = dtv * neg_a
        acum = jnp.dot(mask.astype(F32), a, precision=HIGHEST, preferred_element_type=F32)
        tot = acum[CHUNK - 1:CHUNK, :] if d == 0 else acum[0:1, :]
        w = jnp.exp(tot - acum) * dtv
        cdec = jnp.exp(tot)
        acum_t = acum.T
        dtv_t = dtv.T
        w_t = w.T
        for g in range(GROUPS):
            bg = b_ref[0, :, g * STATE:(g + 1) * STATE]
            cg = c_ref[0, :, g * STATE:(g + 1) * STATE]
            cbm = lax.dot_general(cg, bg, (((1,), (1,)), ((), ())),
                                  preferred_element_type=F32)
            bg_t = bg.astype(F32).T
            for q in range(2):
                cols = slice((2 * g + q) * LANES, (2 * g + q + 1) * LANES)
                xp = x_ref[0, :, cols]
                zero = jnp.zeros_like(xp)
                rhs = jnp.concatenate([jnp.where(lane_lo, xp, zero),
                                       jnp.where(lane_lo, zero, xp)], axis=0)
                lane0 = d * HEADS + g * 4 + q * 2
                m_parts, s_parts, e_parts = [], [], []
                for r2 in range(2):
                    ln = lane0 + r2
                    colb = jnp.broadcast_to(acum[:, ln:ln + 1], (CHUNK, CHUNK))
                    seg = colb - acum_t[ln:ln + 1, :]
                    dec = jnp.exp(jnp.where(mask, seg, -jnp.inf))
                    m_parts.append((cbm * dec * dtv_t[ln:ln + 1, :]).astype(BF16))
                    s_parts.append((bg_t * w_t[ln:ln + 1, :]).astype(BF16))
                    e_parts.append(jnp.exp(colb))
                lhs_y = jnp.concatenate(m_parts, axis=1)
                lhs_s = jnp.concatenate(s_parts, axis=1)
                h_t = st_ref[0, g, :, q * LANES:(q + 1) * LANES]
                y_diag = jnp.dot(lhs_y, rhs, preferred_element_type=F32)
                y_off = (jnp.dot(cg, h_t.astype(BF16), preferred_element_type=F32)
                         * jnp.where(lane_lo, e_parts[0], e_parts[1]))
                y_ref[0, :, cols] = (y_diag + y_off).astype(y_ref.dtype)
                cd = jnp.where(lane_lo[0:1, :],
                               jnp.broadcast_to(cdec[:, lane0:lane0 + 1], (1, LANES)),
                               jnp.broadcast_to(cdec[:, lane0 + 1:lane0 + 2], (1, LANES)))
                st_ref[0, g, :, q * LANES:(q + 1) * LANES] = (
                    h_t * cd + jnp.dot(lhs_s, rhs, preferred_element_type=F32))


def _ssd(xbc, dt3, dt_bias, a_log, h0f, h0b):
    b, l, _ = xbc.shape
    nc = l // CHUNK
    fwd = lambda k: (lambda i, c: (i, c, k))
    bwd = lambda k: (lambda i, c: (i, nc - 1 - c, k))
    xspec = lambda f: pl.BlockSpec((1, CHUNK, D), f(0))
    bspec = lambda f: pl.BlockSpec((1, CHUNK, GROUPS * STATE), f(2))
    cspec = lambda f: pl.BlockSpec((1, CHUNK, GROUPS * STATE), f(3))
    dspec = lambda f: pl.BlockSpec((1, CHUNK, LANES), f(0))
    vec = pl.BlockSpec((1, LANES), lambda i, c: (0, 0))
    st = pl.BlockSpec((1, GROUPS, STATE, 4 * HEAD_DIM), lambda i, c: (i, 0, 0, 0))
    pad = lambda v: jnp.pad(v.reshape(1, N_DT), ((0, 0), (0, LANES - N_DT)))
    return pl.pallas_call(
        _ssd_kernel,
        grid=(b, nc),
        in_specs=[xspec(fwd), bspec(fwd), cspec(fwd), dspec(fwd),
                  xspec(bwd), bspec(bwd), cspec(bwd), dspec(bwd),
                  vec, vec, st, st],
        out_specs=[xspec(fwd), xspec(bwd), st, st],
        out_shape=[jax.ShapeDtypeStruct((b, l, D), BF16),
                   jax.ShapeDtypeStruct((b, l, D), BF16),
                   jax.ShapeDtypeStruct(h0f.shape, F32),
                   jax.ShapeDtypeStruct(h0b.shape, F32)],
        compiler_params=_cparams("arbitrary", "arbitrary"),
        name="ssd",
    )(xbc, xbc, xbc, dt3, xbc, xbc, xbc, dt3, pad(dt_bias), pad(a_log), h0f, h0b)


def _mixer_kernel(yf_ref, yb_ref, xs_ref, z_ref, u_ref, v_ref, ga_ref, gb_ref, h_ref,
                  g1_ref, sc2_ref, sh2_ref, dskip_ref, snw_ref, lnw_ref, lnb_ref, n2w_ref,
                  ws_ref, bs_ref, wpa_ref, wpb_ref, wo_ref, rwt_ref,
                  hout_ref, xf_ref, aff_ref):
    tm = h_ref.shape[1]
    y = (yf_ref[0].astype(F32) + yb_ref[0].astype(F32)
         + dskip_ref[...] * xs_ref[0].astype(F32))
    gated = y * jax.nn.silu(z_ref[0].astype(F32))
    gw = D // GROUPS
    ssd = jnp.concatenate(
        [_rms(gated[:, k * gw:(k + 1) * gw], snw_ref[:, k * gw:(k + 1) * gw])
         for k in range(GROUPS)], axis=1)
    u = jax.nn.gelu(u_ref[0].astype(F32))
    v = jax.nn.gelu(v_ref[0].astype(F32))
    mu = jnp.mean(v, axis=-1, keepdims=True)
    var = jnp.mean(jnp.square(v - mu), axis=-1, keepdims=True)
    vn = ((v - mu) * lax.rsqrt(var + EPS) * lnw_ref[...] + lnb_ref[...]).astype(BF16)
    rows = []
    for ck in range(tm // CHUNK):
        rs = slice(ck * CHUNK, (ck + 1) * CHUNK)
        rows.append(jnp.concatenate(
            [jnp.dot(ws_ref[g], vn[rs, g * LANES:(g + 1) * LANES], preferred_element_type=F32)
             for g in range(MLP_GROUPS)], axis=1) + bs_ref[...])
    mixed = jnp.concatenate(rows, axis=0) if len(rows) > 1 else rows[0]
    sgu = u * mixed
    pa = jnp.dot(ssd.astype(BF16), wpa_ref[...], preferred_element_type=F32)
    pb = jnp.dot(sgu.astype(BF16), wpb_ref[...], preferred_element_type=F32)
    merged = (jax.nn.sigmoid(ga_ref[0].astype(F32)) * pa
              + jax.nn.sigmoid(gb_ref[0].astype(F32)) * pb)
    out = jnp.dot(merged.astype(BF16), wo_ref[...], preferred_element_type=F32)
    hn = h_ref[0] + g1_ref[0] * out
    hout_ref[0] = hn
    xf = _rms(hn, n2w_ref[...]) * (1.0 + sc2_ref[0]) + sh2_ref[0]
    xf_ref[0] = xf.astype(BF16)
    logits = lax.dot_general(rwt_ref[...], xf, (((1,), (1,)), ((), ())),
                             precision=HIGHEST, preferred_element_type=F32)
    ex = jnp.exp(logits - jnp.max(logits, axis=0, keepdims=True))
    aff_ref[0] = ex / jnp.sum(ex, axis=0, keepdims=True)


def _mixer(yf, yb, xbc, p3, h3, g1, sc2, sh2, consts):
    b, l, _ = h3.shape
    tm = min(256, l)
    tok = lambda k: pl.BlockSpec((1, tm, D), lambda i, t: (i, t, k))
    mod = pl.BlockSpec((1, 1, D), (lambda i, t: (i, 0, 0)) if g1.shape[0] > 1
                       else (lambda i, t: (0, 0, 0)))
    vec = pl.BlockSpec((1, D), lambda i, t: (0, 0))
    wmat = pl.BlockSpec((D, D), lambda i, t: (0, 0))
    return pl.pallas_call(
        _mixer_kernel,
        grid=(b, l // tm),
        in_specs=[tok(0), tok(0), tok(0), tok(2), tok(3), tok(4), tok(5), tok(6), tok(0),
                  mod, mod, mod, vec, vec, vec, vec, vec,
                  pl.BlockSpec((MLP_GROUPS, CHUNK, CHUNK), lambda i, t: (0, 0, 0)),
                  pl.BlockSpec((CHUNK, D), lambda i, t: (0, 0)),
                  wmat, wmat, wmat,
                  pl.BlockSpec((N_EXPERTS, D), lambda i, t: (0, 0))],
        out_specs=[tok(0), tok(0),
                   pl.BlockSpec((1, N_EXPERTS, tm), lambda i, t: (i, 0, t))],
        out_shape=[jax.ShapeDtypeStruct((b, l, D), F32),
                   jax.ShapeDtypeStruct((b, l, D), BF16),
                   jax.ShapeDtypeStruct((b, N_EXPERTS, l), F32)],
        compiler_params=_cparams("arbitrary", "arbitrary"),
        name="mixer",
    )(yf, yb, xbc, p3, p3, p3, p3, p3, h3, g1, sc2, sh2, *consts)


def _moe_kernel(xf_ref, idxc_ref, idxr_ref, gate_ref, wg_ref, wu_ref, wd_ref, o_ref):
    bg, n, _ = xf_ref.shape
    cap = idxc_ref.shape[2]

    @pl.when(pl.program_id(1) == 0)
    def _():
        o_ref[...] = jnp.zeros_like(o_ref)

    tok_l = lax.broadcasted_iota(jnp.int32, (cap, n), 1)
    tok_s = lax.broadcasted_iota(jnp.int32, (n, cap), 0)
    parts = []
    for k in range(bg):
        onehot = (idxc_ref[k, 0] == tok_l).astype(BF16)
        parts.append(jnp.dot(onehot, xf_ref[k], preferred_element_type=F32).astype(BF16))
    xg = jnp.concatenate(parts, axis=0) if bg > 1 else parts[0]
    hid = (jax.nn.silu(jnp.dot(xg, wg_ref[0], preferred_element_type=F32))
           * jnp.dot(xg, wu_ref[0], preferred_element_type=F32))
    y = jnp.dot(hid.astype(BF16), wd_ref[0], preferred_element_type=F32)
    for k in range(bg):
        yk = (y[k * cap:(k + 1) * cap] * gate_ref[k, 0]).astype(BF16)
        onehot_t = (tok_s == idxr_ref[k, 0]).astype(BF16)
        o_ref[k] += jnp.dot(onehot_t, yk, preferred_element_type=F32)


def _moe(xf, idx, gate, wg, wu, wd, bg):
    b, n, _ = xf.shape
    cap = idx.shape[2]
    tokens = pl.BlockSpec((bg, n, D), lambda i, e: (i, 0, 0))
    wspec = pl.BlockSpec((1, D, D), lambda i, e: (e, 0, 0))
    return pl.pallas_call(
        _moe_kernel,
        grid=(b // bg, N_EXPERTS),
        in_specs=[tokens,
                  pl.BlockSpec((bg, 1, cap, 1), lambda i, e: (i, e, 0, 0)),
                  pl.BlockSpec((bg, 1, 1, cap), lambda i, e: (i, e, 0, 0)),
                  pl.BlockSpec((bg, 1, cap, 1), lambda i, e: (i, e, 0, 0)),
                  wspec, wspec, wspec],
        out_specs=tokens,
        out_shape=jax.ShapeDtypeStruct((b, n, D), F32),
        compiler_params=_cparams("arbitrary", "arbitrary"),
        name="moe",
    )(xf, idx[..., None], idx[:, :, None, :], gate[..., None], wg, wu, wd)


def _final_kernel(h_ref, moe_ref, g_ref, w_ref, o_ref):
    o_ref[0] = _rms(h_ref[0] + g_ref[0] * moe_ref[0], w_ref[...])


def _final(h3, moe3, g2, w):
    b, l, _ = h3.shape
    tm = min(512, l)
    tok = pl.BlockSpec((1, tm, D), lambda i, t: (i, t, 0))
    return pl.pallas_call(
        _final_kernel,
        grid=(b, l // tm),
        in_specs=[tok, tok, pl.BlockSpec((1, 1, D), lambda i, t: (i, 0, 0)),
                  pl.BlockSpec((1, D), lambda i, t: (0, 0))],
        out_specs=tok,
        out_shape=jax.ShapeDtypeStruct(h3.shape, F32),
        compiler_params=_cparams("arbitrary", "arbitrary"),
        name="final",
    )(h3, moe3, g2, w.reshape(1, D))


def _route(aff, n):
    cap = CAPACITY_FACTOR * n // N_EXPERTS
    gate, idx = lax.top_k(aff, cap)
    return idx.astype(jnp.int32), gate


def kernel(x, c, ctx, c_ctx, ada_w, ada_b, norm1_w, norm2_w, w_in, conv_w, conv_b, dt_bias, a_log,
           d_skip, ssd_norm_w, sgu_ln_w, sgu_ln_b, w_s, b_s, w_pa, w_pb, w_o, router_w,
           w_gate, w_up, w_down, final_norm_w):
    bsz, n_lat, _ = x.shape
    n_ctx = ctx.shape[1]
    depth = ada_w.shape[0]

    mod_rows = ((bsz + 1 + 7) // 8) * 8
    cc = jnp.zeros((mod_rows, D), F32).at[:bsz].set(c).at[bsz].set(c_ctx)
    mods = _ada(cc, ada_w, ada_b)

    zero_state = jnp.zeros((bsz, GROUPS, STATE, 4 * HEAD_DIM), F32)
    h = x.reshape(bsz * n_lat, D)
    hc = ctx.reshape(bsz * n_ctx, D)
    pending_l = None
    pending_c = None
    for layer in range(depth):
        last = layer == depth - 1
        ml = [mods[layer, :bsz, k * D:(k + 1) * D].reshape(bsz, 1, D) for k in range(6)]
        mc = [mods[layer, bsz:bsz + 1, k * D:(k + 1) * D].reshape(1, 1, D) for k in range(6)]
        sh1_l, sc1_l, g1_l, sh2_l, sc2_l, g2_l = ml
        sh1_c, sc1_c, g1_c, sh2_c, sc2_c, g2_c = mc

        w_l = w_in[layer]
        w_main = jnp.concatenate([w_l[:, :N_XBC], w_l[:, N_SCAN:]], axis=1).astype(BF16)
        w_dt = jnp.pad(w_l[:, N_XBC:N_SCAN], ((0, 0), (0, LANES - N_DT))).astype(BF16)
        consts = (jnp.repeat(d_skip[layer], HEAD_DIM).reshape(1, D),
                  ssd_norm_w[layer].reshape(1, D),
                  sgu_ln_w[layer].reshape(1, D), sgu_ln_b[layer].reshape(1, D),
                  norm2_w[layer].reshape(1, D),
                  w_s[layer].astype(BF16),
                  jnp.repeat(b_s[layer].T, LANES, axis=1),
                  w_pa[layer].astype(BF16), w_pb[layer].astype(BF16), w_o[layer].astype(BF16),
                  router_w[layer].T)
        wg = w_gate[layer].astype(BF16)
        wu = w_up[layer].astype(BF16)
        wd = w_down[layer].astype(BF16)

        wc = w_main[:, :N_XBC] if last else w_main
        res_c = _in_proj(hc, sc1_c, sh1_c, norm1_w[layer], wc, w_dt, bsz * n_ctx, res=pending_c)
        if pending_c is not None:
            p_c, dt_c, hc = res_c
        else:
            p_c, dt_c = res_c
        p_c3 = p_c.reshape(bsz, n_ctx, -1)
        xbc_c = _conv(p_c3, conv_w[layer], conv_b[layer])
        yf_c, yb_c, st_f, st_b = _ssd(xbc_c, dt_c.reshape(bsz, n_ctx, LANES),
                                      dt_bias[layer], a_log[layer], zero_state, zero_state)

        res_l = _in_proj(h, sc1_l, sh1_l, norm1_w[layer], w_main, w_dt, n_lat, res=pending_l)
        if pending_l is not None:
            p_l, dt_l, h = res_l
        else:
            p_l, dt_l = res_l
        p_l3 = p_l.reshape(bsz, n_lat, N_MAIN)
        xbc_l = _conv(p_l3, conv_w[layer], conv_b[layer])
        yf_l, yb_l, _, _ = _ssd(xbc_l, dt_l.reshape(bsz, n_lat, LANES),
                                dt_bias[layer], a_log[layer], st_f, st_b)
        h3, xf_l, aff_l = _mixer(yf_l, yb_l, xbc_l, p_l3, h.reshape(bsz, n_lat, D),
                                 g1_l, sc2_l, sh2_l, consts)
        idx_l, gate_l = _route(aff_l, n_lat)
        moe_l = _moe(xf_l, idx_l, gate_l, wg, wu, wd, bg=1)
        h = h3.reshape(bsz * n_lat, D)
        pending_l = (moe_l.reshape(bsz * n_lat, D), g2_l)

        if not last:
            hc3, xf_c, aff_c = _mixer(yf_c, yb_c, xbc_c, p_c3, hc.reshape(bsz, n_ctx, D),
                                      g1_c, sc2_c, sh2_c, consts)
            idx_c, gate_c = _route(aff_c, n_ctx)
            moe_c = _moe(xf_c, idx_c, gate_c, wg, wu, wd, bg=min(8, bsz))
            hc = hc3.reshape(bsz * n_ctx, D)
            pending_c = (moe_c.reshape(bsz * n_ctx, D), g2_c)

    moe_l, g2_l = pending_l
    return _final(h.reshape(bsz, n_lat, D), moe_l.reshape(bsz, n_lat, D), g2_l, final_norm_w)
```

```python
import functools

import jax
import jax.numpy as jnp
from jax import lax
from jax.experimental import pallas as pl
from jax.experimental.pallas import tpu as pltpu

F32 = jnp.float32
BF16 = jnp.bfloat16
HIGHEST = lax.Precision.HIGHEST

D = 1024
CHUNK = 128
EPS = 1e-6
HEADS = 16
HEAD_DIM = 64
GROUPS = 4
STATE = 128
N_XBC = 2048
N_DT = 2 * HEADS
N_SCAN = N_XBC + N_DT
N_MAIN = 7 * D
MLP_GROUPS = 8
N_EXPERTS = 16
CAPACITY_FACTOR = 2
LANES = 128
VMEM_LIMIT = 56 * 1024 * 1024


def _cparams(*sem):
    return pltpu.CompilerParams(dimension_semantics=sem, vmem_limit_bytes=VMEM_LIMIT)


def _softplus(x):
    return jnp.maximum(x, 0.0) + jnp.log1p(jnp.exp(-jnp.abs(x)))


def _rms(x, w):
    return x * lax.rsqrt(jnp.mean(x * x, axis=-1, keepdims=True) + EPS) * w


def _cast_kernel(w_ref, o_ref):
    o_ref[...] = w_ref[0].astype(o_ref.dtype)


def _to_bf16(w, layer):
    g = w.shape[1]
    return pl.pallas_call(
        _cast_kernel,
        grid=(g, 2),
        in_specs=[pl.BlockSpec((1, 1, D // 2, D), lambda i, j: (layer, i, j, 0))],
        out_specs=pl.BlockSpec((1, D // 2, D), lambda i, j: (i, j, 0)),
        out_shape=jax.ShapeDtypeStruct(w.shape[1:], BF16),
        compiler_params=_cparams("arbitrary", "arbitrary"),
        name="to_bf16",
    )(w)


def _ada_kernel(c_ref, w_ref, b_ref, o_ref):
    a = jax.nn.silu(c_ref[...]).astype(BF16)
    o_ref[0] = jnp.dot(a, w_ref[0].astype(BF16), preferred_element_type=F32) + b_ref[0]


def _ada(cc, ada_w, ada_b):
    n_layers = ada_w.shape[0]
    rows = cc.shape[0]
    return pl.pallas_call(
        _ada_kernel,
        grid=(n_layers, 6),
        in_specs=[pl.BlockSpec((rows, D), lambda l, j: (0, 0)),
                  pl.BlockSpec((1, D, D), lambda l, j: (l, 0, j)),
                  pl.BlockSpec((1, 1, D), lambda l, j: (l, 0, j))],
        out_specs=pl.BlockSpec((1, rows, D), lambda l, j: (l, 0, j)),
        out_shape=jax.ShapeDtypeStruct((n_layers, rows, 6 * D), F32),
        compiler_params=_cparams("arbitrary", "arbitrary"),
        name="ada",
    )(cc, ada_w, ada_b.reshape(n_layers, 1, 6 * D))


def _in_proj_kernel(with_res, *refs):
    if with_res:
        (h_ref, moe_ref, g_ref, sc_ref, sh_ref, nw_ref, w_ref, wdt_ref,
         p_ref, dt_ref, hout_ref) = refs
    else:
        h_ref, sc_ref, sh_ref, nw_ref, w_ref, wdt_ref, p_ref, dt_ref = refs
    h = h_ref[...]
    if with_res:
        h = h + g_ref[0] * moe_ref[...]
        hout_ref[...] = h
    xm = (_rms(h, nw_ref[...]) * (1.0 + sc_ref[0]) + sh_ref[0]).astype(BF16)
    dt_ref[...] = jnp.dot(xm, wdt_ref[...], preferred_element_type=F32)
    for j in range(p_ref.shape[1] // D):
        cols = slice(j * D, (j + 1) * D)
        p_ref[:, cols] = jnp.dot(xm, w_ref[:, cols], preferred_element_type=F32).astype(BF16)


def _in_proj(h2, sc, sh, norm_w, w_main, w_dt, tokens_per_mod, res=None):
    m = h2.shape[0]
    ncols = w_main.shape[1]
    tm = min(512, tokens_per_mod)
    tiles_per_mod = tokens_per_mod // tm
    with_res = res is not None
    row = pl.BlockSpec((tm, D), lambda i: (i, 0))
    mod = pl.BlockSpec((1, 1, D), lambda i: (i // tiles_per_mod, 0, 0))
    resident = lambda shape: pl.BlockSpec(shape, lambda i: (0, 0), pipeline_mode=pl.Buffered(1))
    in_specs = [row]
    args = [h2]
    if with_res:
        in_specs += [row, mod]
        args += [res[0], res[1]]
    in_specs += [mod, mod, resident((1, D)), resident((D, ncols)), resident((D, LANES))]
    args += [sc, sh, norm_w.reshape(1, D), w_main, w_dt]
    out_specs = [pl.BlockSpec((tm, ncols), lambda i: (i, 0)),
                 pl.BlockSpec((tm, LANES), lambda i: (i, 0))]
    out_shape = [jax.ShapeDtypeStruct((m, ncols), BF16),
                 jax.ShapeDtypeStruct((m, LANES), F32)]
    if with_res:
        out_specs.append(row)
        out_shape.append(jax.ShapeDtypeStruct((m, D), F32))
    return pl.pallas_call(
        functools.partial(_in_proj_kernel, with_res),
        grid=(m // tm,),
        in_specs=in_specs, out_specs=out_specs, out_shape=out_shape,
        compiler_params=_cparams("arbitrary"),
        name="in_proj_res" if with_res else "in_proj",
    )(*args)


def _conv_kernel(p_ref, w_ref, b_ref, o_ref):
    x = p_ref[0].astype(F32)
    n = x.shape[0]
    row = lax.broadcasted_iota(jnp.int32, x.shape, 0)
    prev = jnp.where(row == 0, 0.0, pltpu.roll(x, 1, 0))
    nxt = jnp.where(row == n - 1, 0.0, pltpu.roll(x, n - 1, 0))
    y = prev * w_ref[0:1, :] + x * w_ref[1:2, :] + nxt * w_ref[2:3, :] + b_ref[...]
    o_ref[0] = jax.nn.silu(y).astype(o_ref.dtype)


def _conv(p3, conv_w, conv_b):
    b, l, _ = p3.shape
    tc = 256
    return pl.pallas_call(
        _conv_kernel,
        grid=(b, N_XBC // tc),
        in_specs=[pl.BlockSpec((1, l, tc), lambda i, j: (i, 0, j)),
                  pl.BlockSpec((3, tc), lambda i, j: (0, j)),
                  pl.BlockSpec((1, tc), lambda i, j: (0, j))],
        out_specs=pl.BlockSpec((1, l, tc), lambda i, j: (i, 0, j)),
        out_shape=jax.ShapeDtypeStruct((b, l, N_XBC), BF16),
        compiler_params=_cparams("arbitrary", "arbitrary"),
        name="conv",
    )(p3, conv_w, conv_b.reshape(1, N_XBC))


def _dtprep_kernel(dt_ref, dtbias_ref, alog_ref, acum_ref, rows_ref):
    nc = rows_ref.shape[1]
    ri = lax.broadcasted_iota(jnp.int32, (CHUNK, CHUNK), 0)
    ci = lax.broadcasted_iota(jnp.int32, (CHUNK, CHUNK), 1)
    tri_f = (ci <= ri).astype(F32)
    tri_b = (ci >= ri).astype(F32)
    fwd_lane = lax.broadcasted_iota(jnp.int32, (CHUNK, LANES), 1) < HEADS
    neg_a = -jnp.exp(alog_ref[...])
    for c in range(nc):
        rs = slice(c * CHUNK, (c + 1) * CHUNK)
        dtv = _softplus(dt_ref[0, rs, :] + dtbias_ref[...])
        a = dtv * neg_a
        acum = jnp.where(
            fwd_lane,
            jnp.dot(tri_f, a, precision=HIGHEST, preferred_element_type=F32),
            jnp.dot(tri_b, a, precision=HIGHEST, preferred_element_type=F32))
        tot = jnp.where(fwd_lane[0:1, :], acum[CHUNK - 1:CHUNK, :], acum[0:1, :])
        w = jnp.exp(tot - acum) * dtv
        acum_ref[0, rs, :] = acum
        rows_ref[0, c, 0:N_DT, :] = acum.T[0:N_DT, :]
        rows_ref[0, c, N_DT:2 * N_DT, :] = dtv.T[0:N_DT, :]
        rows_ref[0, c, 2 * N_DT:3 * N_DT, :] = w.T[0:N_DT, :]


def _dtprep(dt3, dt_bias, a_log):
    b, l, _ = dt3.shape
    nc = l // CHUNK
    pad = lambda v: jnp.pad(v.reshape(1, N_DT), ((0, 0), (0, LANES - N_DT)))
    vec = pl.BlockSpec((1, LANES), lambda i: (0, 0))
    return pl.pallas_call(
        _dtprep_kernel,
        grid=(b,),
        in_specs=[pl.BlockSpec((1, l, LANES), lambda i: (i, 0, 0)), vec, vec],
        out_specs=[pl.BlockSpec((1, l, LANES), lambda i: (i, 0, 0)),
                   pl.BlockSpec((1, nc, 3 * N_DT, CHUNK), lambda i: (i, 0, 0, 0))],
        out_shape=[jax.ShapeDtypeStruct((b, l, LANES), F32),
                   jax.ShapeDtypeStruct((b, nc, 3 * N_DT, CHUNK), F32)],
        compiler_params=_cparams("arbitrary"),
        name="dtprep",
    )(dt3, pad(dt_bias), pad(a_log))


def _ssd_kernel(xf_ref, bf_ref, cf_ref, af_ref, rf_ref, xb_ref, bb_ref, cb_ref, ab_ref, rb_ref,
                h0f_ref, h0b_ref, yf_ref, yb_ref, stf_ref, stb_ref):
    @pl.when(pl.program_id(1) == 0)
    def _():
        stf_ref[...] = h0f_ref[...]
        stb_ref[...] = h0b_ref[...]

    ri = lax.broadcasted_iota(jnp.int32, (CHUNK, CHUNK), 0)
    ci = lax.broadcasted_iota(jnp.int32, (CHUNK, CHUNK), 1)
    lane_lo = ci < HEAD_DIM
    streams = ((xf_ref, bf_ref, cf_ref, af_ref, rf_ref, yf_ref, stf_ref),
               (xb_ref, bb_ref, cb_ref, ab_ref, rb_ref, yb_ref, stb_ref))
    for d, (x_ref, b_ref, c_ref, a_ref, r_ref, y_ref, st_ref) in enumerate(streams):
        mask = (ci <= ri) if d == 0 else (ci >= ri)
        acum = a_ref[0]
        tot = acum[CHUNK - 1:CHUNK, :] if d == 0 else acum[0:1, :]
        cdec = jnp.exp(tot)
        acum_t = r_ref[0, 0, 0:N_DT, :]
        dtv_t = r_ref[0, 0, N_DT:2 * N_DT, :]
        w_t = r_ref[0, 0, 2 * N_DT:3 * N_DT, :]
        for g in range(GROUPS):
            bg = b_ref[0, :, g * STATE:(g + 1) * STATE]
            cg = c_ref[0, :, g * STATE:(g + 1) * STATE]
            cbm = lax.dot_general(cg, bg, (((1,), (1,)), ((), ())),
                                  preferred_element_type=F32)
            bg_t = bg.astype(F32).T
            for q in range(2):
                cols = slice((2 * g + q) * LANES, (2 * g + q + 1) * LANES)
                xp = x_ref[0, :, cols]
                zero = jnp.zeros_like(xp)
                rhs = jnp.concatenate([jnp.where(lane_lo, xp, zero),
                                       jnp.where(lane_lo, zero, xp)], axis=0)
                lane0 = d * HEADS + g * 4 + q * 2
                m_parts, s_parts, e_parts = [], [], []
                for r2 in range(2):
                    ln = lane0 + r2
                    colb = jnp.broadcast_to(acum[:, ln:ln + 1], (CHUNK, CHUNK))
                    seg = colb - acum_t[ln:ln + 1, :]
                    dec = jnp.exp(jnp.where(mask, seg, -jnp.inf))
                    m_parts.append((cbm * dec * dtv_t[ln:ln + 1, :]).astype(BF16))
                    s_parts.append((bg_t * w_t[ln:ln + 1, :]).astype(BF16))
                    e_parts.append(jnp.exp(colb))
                lhs_y = jnp.concatenate(m_parts, axis=1)
                lhs_s = jnp.concatenate(s_parts, axis=1)
                h_t = st_ref[0, g, :, q * LANES:(q + 1) * LANES]
                y_diag = jnp.dot(lhs_y, rhs, preferred_element_type=F32)
                y_off = (jnp.dot(cg, h_t.astype(BF16), preferred_element_type=F32)
                         * jnp.where(lane_lo, e_parts[0], e_parts[1]))
                y_ref[0, :, cols] = (y_diag + y_off).astype(y_ref.dtype)
                cd = jnp.where(lane_lo[0:1, :],
                               jnp.broadcast_to(cdec[:, lane0:lane0 + 1], (1, LANES)),
                               jnp.broadcast_to(cdec[:, lane0 + 1:lane0 + 2], (1, LANES)))
                st_ref[0, g, :, q * LANES:(q + 1) * LANES] = (
                    h_t * cd + jnp.dot(lhs_s, rhs, preferred_element_type=F32))


def _ssd(xbc, dt3, dt_bias, a_log, h0f, h0b):
    b, l, _ = xbc.shape
    nc = l // CHUNK
    acum, rows = _dtprep(dt3, dt_bias, a_log)
    fwd = lambda k: (lambda i, c: (i, c, k))
    bwd = lambda k: (lambda i, c: (i, nc - 1 - c, k))
    xspec = lambda f: pl.BlockSpec((1, CHUNK, D), f(0))
    bspec = lambda f: pl.BlockSpec((1, CHUNK, GROUPS * STATE), f(2))
    cspec = lambda f: pl.BlockSpec((1, CHUNK, GROUPS * STATE), f(3))
    dspec = lambda f: pl.BlockSpec((1, CHUNK, LANES), f(0))
    rspec = lambda rev: pl.BlockSpec(
        (1, 1, 3 * N_DT, CHUNK),
        (lambda i, c: (i, nc - 1 - c, 0, 0)) if rev else (lambda i, c: (i, c, 0, 0)))
    st = pl.BlockSpec((1, GROUPS, STATE, 4 * HEAD_DIM), lambda i, c: (i, 0, 0, 0))
    return pl.pallas_call(
        _ssd_kernel,
        grid=(b, nc),
        in_specs=[xspec(fwd), bspec(fwd), cspec(fwd), dspec(fwd), rspec(False),
                  xspec(bwd), bspec(bwd), cspec(bwd), dspec(bwd), rspec(True),
                  st, st],
        out_specs=[xspec(fwd), xspec(bwd), st, st],
        out_shape=[jax.ShapeDtypeStruct((b, l, D), BF16),
                   jax.ShapeDtypeStruct((b, l, D), BF16),
                   jax.ShapeDtypeStruct(h0f.shape, F32),
                   jax.ShapeDtypeStruct(h0b.shape, F32)],
        compiler_params=_cparams("arbitrary", "arbitrary"),
        name="ssd",
    )(xbc, xbc, xbc, acum, rows, xbc, xbc, xbc, acum, rows, h0f, h0b)


def _split_bf16(x):
    hi = x.astype(BF16)
    return hi, (x - hi.astype(F32)).astype(BF16)


def _mixer_kernel(sub, yf_ref, yb_ref, xs_ref, z_ref, u_ref, v_ref, ga_ref, gb_ref, h_ref,
                  g1_ref, sc2_ref, sh2_ref, dskip_ref, snw_ref, lnw_ref, lnb_ref, n2w_ref,
                  ws_ref, bs_ref, wpa_ref, wpb_ref, wo_ref, rwh_ref, rwl_ref,
                  hout_ref, xf_ref, afft_ref):
    tm = h_ref.shape[1]
    gw = D // GROUPS
    lane = lax.broadcasted_iota(jnp.int32, (sub, LANES), 1)
    for s in range(tm // sub):
        rs = slice(s * sub, (s + 1) * sub)
        y = (yf_ref[0, rs, :].astype(F32) + yb_ref[0, rs, :].astype(F32)
             + dskip_ref[...] * xs_ref[0, rs, :].astype(F32))
        gated = y * jax.nn.silu(z_ref[0, rs, :].astype(F32))
        ssd = jnp.concatenate(
            [_rms(gated[:, k * gw:(k + 1) * gw], snw_ref[:, k * gw:(k + 1) * gw])
             for k in range(GROUPS)], axis=1)
        u = jax.nn.gelu(u_ref[0, rs, :].astype(F32))
        v = jax.nn.gelu(v_ref[0, rs, :].astype(F32))
        mu = jnp.mean(v, axis=-1, keepdims=True)
        var = jnp.mean(jnp.square(v - mu), axis=-1, keepdims=True)
        vn = ((v - mu) * lax.rsqrt(var + EPS) * lnw_ref[...] + lnb_ref[...]).astype(BF16)
        rows = []
        for ck in range(sub // CHUNK):
            cs = slice(ck * CHUNK, (ck + 1) * CHUNK)
            rows.append(jnp.concatenate(
                [jnp.dot(ws_ref[g], vn[cs, g * LANES:(g + 1) * LANES],
                         preferred_element_type=F32)
                 for g in range(MLP_GROUPS)], axis=1) + bs_ref[...])
        mixed = jnp.concatenate(rows, axis=0) if len(rows) > 1 else rows[0]
        sgu = u * mixed
        pa = jnp.dot(ssd.astype(BF16), wpa_ref[...], preferred_element_type=F32)
        pb = jnp.dot(sgu.astype(BF16), wpb_ref[...], preferred_element_type=F32)
        merged = (jax.nn.sigmoid(ga_ref[0, rs, :].astype(F32)) * pa
                  + jax.nn.sigmoid(gb_ref[0, rs, :].astype(F32)) * pb)
        out = jnp.dot(merged.astype(BF16), wo_ref[...], preferred_element_type=F32)
        hn = h_ref[0, rs, :] + g1_ref[0] * out
        hout_ref[0, rs, :] = hn
        xf = _rms(hn, n2w_ref[...]) * (1.0 + sc2_ref[0]) + sh2_ref[0]
        xf_ref[0, rs, :] = xf.astype(BF16)
        xh, xl = _split_bf16(xf)
        logits = (jnp.dot(xh, rwh_ref[...], preferred_element_type=F32)
                  + jnp.dot(xl, rwh_ref[...], preferred_element_type=F32)
                  + jnp.dot(xh, rwl_ref[...], preferred_element_type=F32))
        logits = jnp.where(lane < N_EXPERTS, logits, -jnp.inf)
        ex = jnp.exp(logits - jnp.max(logits, axis=1, keepdims=True))
        aff = ex / jnp.sum(ex, axis=1, keepdims=True)
        afft_ref[0, :, rs] = aff.T[:N_EXPERTS, :]


def _mixer(yf, yb, xbc, p3, h3, g1, sc2, sh2, consts):
    b, l, _ = h3.shape
    sub = min(256, l)
    tm = min(512, l)
    tok = lambda k: pl.BlockSpec((1, tm, D), lambda i, t: (i, t, k))
    mod = pl.BlockSpec((1, 1, D), (lambda i, t: (i, 0, 0)) if g1.shape[0] > 1
                       else (lambda i, t: (0, 0, 0)))
    const = lambda shape: pl.BlockSpec(shape, lambda i, t: (0,) * len(shape),
                                       pipeline_mode=pl.Buffered(1))
    vec = const((1, D))
    wmat = const((D, D))
    return pl.pallas_call(
        functools.partial(_mixer_kernel, sub),
        grid=(b, l // tm),
        in_specs=[tok(0), tok(0), tok(0), tok(2), tok(3), tok(4), tok(5), tok(6), tok(0),
                  mod, mod, mod, vec, vec, vec, vec, vec,
                  const((MLP_GROUPS, CHUNK, CHUNK)), const((CHUNK, D)),
                  wmat, wmat, wmat, const((D, LANES)), const((D, LANES))],
        out_specs=[tok(0), tok(0),
                   pl.BlockSpec((1, N_EXPERTS, tm), lambda i, t: (i, 0, t))],
        out_shape=[jax.ShapeDtypeStruct((b, l, D), F32),
                   jax.ShapeDtypeStruct((b, l, D), BF16),
                   jax.ShapeDtypeStruct((b, N_EXPERTS, l), F32)],
        compiler_params=_cparams("arbitrary", "arbitrary"),
        name="mixer",
    )(yf, yb, xbc, p3, p3, p3, p3, p3, h3, g1, sc2, sh2, *consts)


def _prefix_excl(m):
    n = m.shape[1]
    tri = (lax.broadcasted_iota(jnp.int32, (LANES, LANES), 0)
           <= lax.broadcasted_iota(jnp.int32, (LANES, LANES), 1)).astype(BF16)
    outs = []
    off = jnp.zeros((m.shape[0], 1), F32)
    for k in range(n // LANES):
        mk = m[:, k * LANES:(k + 1) * LANES]
        inc = jnp.dot(mk.astype(BF16), tri, preferred_element_type=F32)
        outs.append(inc - mk + off)
        off = off + inc[:, LANES - 1:LANES]
    return jnp.concatenate(outs, axis=1) if len(outs) > 1 else outs[0]


def _topk_kernel(cap, aff_ref, pos_ref):
    a = aff_ref[0]
    bits = lax.bitcast_convert_type(a, jnp.int32)
    n_exp = a.shape[0]
    count = lambda pred: jnp.sum(jnp.where(pred, 1.0, 0.0), axis=1, keepdims=True)

    def body(_, carry):
        lo, hi = carry
        mid = lo + ((hi - lo + 1) >> 1)
        ok = count(bits >= mid) >= cap
        return jnp.where(ok, mid, lo), jnp.where(ok, hi, mid - 1)

    lo0 = jnp.zeros((n_exp, 1), jnp.int32)
    hi0 = jnp.full((n_exp, 1), 0x7F800000, jnp.int32)
    thr, _ = lax.fori_loop(0, 31, body, (lo0, hi0))
    gt = bits > thr
    eq = bits == thr
    need = cap - count(gt)
    eq_rank = _prefix_excl(jnp.where(eq, 1.0, 0.0))
    sel = gt | (eq & (eq_rank < need))
    slot = _prefix_excl(jnp.where(sel, 1.0, 0.0))
    pos_ref[0] = jnp.where(sel, slot, -1.0).astype(jnp.int32)


def _topk(aff_t):
    b, n_exp, n = aff_t.shape
    cap = CAPACITY_FACTOR * n // N_EXPERTS
    spec = pl.BlockSpec((1, n_exp, n), lambda i: (i, 0, 0))
    return pl.pallas_call(
        functools.partial(_topk_kernel, cap),
        grid=(b,),
        in_specs=[spec], out_specs=spec,
        out_shape=jax.ShapeDtypeStruct((b, n_exp, n), jnp.int32),
        compiler_params=_cparams("arbitrary"),
        name="topk",
    )(aff_t)


def _moe_kernel(cap, xf_ref, aff_ref, pos_ref, wg_ref, wu_ref, wd_ref, o_ref):
    bg, n, _ = xf_ref.shape
    e = pl.program_id(1)

    @pl.when(e == 0)
    def _():
        o_ref[...] = jnp.zeros_like(o_ref)

    slot = lax.broadcasted_iota(jnp.int32, (cap, n), 0)
    onehots, gates, parts = [], [], []
    for k in range(bg):
        hit = pos_ref[k, pl.ds(e, 1), :] == slot
        onehot = hit.astype(BF16)
        onehots.append(onehot)
        parts.append(jnp.dot(onehot, xf_ref[k], preferred_element_type=F32).astype(BF16))
        gates.append(jnp.sum(jnp.where(hit, aff_ref[k, pl.ds(e, 1), :], 0.0),
                             axis=1, keepdims=True))
    xg = jnp.concatenate(parts, axis=0) if bg > 1 else parts[0]
    hid = (jax.nn.silu(jnp.dot(xg, wg_ref[0], preferred_element_type=F32))
           * jnp.dot(xg, wu_ref[0], preferred_element_type=F32))
    y = jnp.dot(hid.astype(BF16), wd_ref[0], preferred_element_type=F32)
    for k in range(bg):
        yk = (y[k * cap:(k + 1) * cap] * gates[k]).astype(BF16)
        o_ref[k] += lax.dot_general(onehots[k], yk, (((0,), (0,)), ((), ())),
                                    preferred_element_type=F32)


def _moe(xf, aff_t, pos, wg, wu, wd, bg):
    b, n, _ = xf.shape
    cap = CAPACITY_FACTOR * n // N_EXPERTS
    tokens = pl.BlockSpec((bg, n, D), lambda i, e: (i, 0, 0))
    routing = pl.BlockSpec((bg, N_EXPERTS, n), lambda i, e: (i, 0, 0))
    wspec = pl.BlockSpec((1, D, D), lambda i, e: (e, 0, 0))
    return pl.pallas_call(
        functools.partial(_moe_kernel, cap),
        grid=(b // bg, N_EXPERTS),
        in_specs=[tokens, routing, routing, wspec, wspec, wspec],
        out_specs=tokens,
        out_shape=jax.ShapeDtypeStruct((b, n, D), F32),
        compiler_params=_cparams("arbitrary", "arbitrary"),
        name="moe",
    )(xf, aff_t, pos, wg, wu, wd)


def _final_kernel(h_ref, moe_ref, g_ref, w_ref, o_ref):
    o_ref[0] = _rms(h_ref[0] + g_ref[0] * moe_ref[0], w_ref[...])


def _final(h3, moe3, g2, w):
    b, l, _ = h3.shape
    tm = min(512, l)
    tok = pl.BlockSpec((1, tm, D), lambda i, t: (i, t, 0))
    return pl.pallas_call(
        _final_kernel,
        grid=(b, l // tm),
        in_specs=[tok, tok, pl.BlockSpec((1, 1, D), lambda i, t: (i, 0, 0)),
                  pl.BlockSpec((1, D), lambda i, t: (0, 0))],
        out_specs=tok,
        out_shape=jax.ShapeDtypeStruct(h3.shape, F32),
        compiler_params=_cparams("arbitrary", "arbitrary"),
        name="final",
    )(h3, moe3, g2, w.reshape(1, D))


def kernel(x, c, ctx, c_ctx, ada_w, ada_b, norm1_w, norm2_w, w_in, conv_w, conv_b, dt_bias, a_log,
           d_skip, ssd_norm_w, sgu_ln_w, sgu_ln_b, w_s, b_s, w_pa, w_pb, w_o, router_w,
           w_gate, w_up, w_down, final_norm_w):
    bsz, n_lat, _ = x.shape
    n_ctx = ctx.shape[1]
    depth = ada_w.shape[0]

    mod_rows = ((bsz + 1 + 7) // 8) * 8
    cc = jnp.zeros((mod_rows, D), F32).at[:bsz].set(c).at[bsz].set(c_ctx)
    mods = _ada(cc, ada_w, ada_b)

    zero_state = jnp.zeros((bsz, GROUPS, STATE, 4 * HEAD_DIM), F32)
    h = x.reshape(bsz * n_lat, D)
    hc = ctx.reshape(bsz * n_ctx, D)
    pending_l = None
    pending_c = None
    for layer in range(depth):
        last = layer == depth - 1
        ml = [mods[layer, :bsz, k * D:(k + 1) * D].reshape(bsz, 1, D) for k in range(6)]
        mc = [mods[layer, bsz:bsz + 1, k * D:(k + 1) * D].reshape(1, 1, D) for k in range(6)]
        sh1_l, sc1_l, g1_l, sh2_l, sc2_l, g2_l = ml
        sh1_c, sc1_c, g1_c, sh2_c, sc2_c, g2_c = mc

        w_l = w_in[layer]
        w_main = jnp.concatenate([w_l[:, :N_XBC], w_l[:, N_SCAN:]], axis=1).astype(BF16)
        w_dt = jnp.pad(w_l[:, N_XBC:N_SCAN], ((0, 0), (0, LANES - N_DT))).astype(BF16)
        consts = (jnp.repeat(d_skip[layer], HEAD_DIM).reshape(1, D),
                  ssd_norm_w[layer].reshape(1, D),
                  sgu_ln_w[layer].reshape(1, D), sgu_ln_b[layer].reshape(1, D),
                  norm2_w[layer].reshape(1, D),
                  w_s[layer].astype(BF16),
                  jnp.repeat(b_s[layer].T, LANES, axis=1),
                  w_pa[layer].astype(BF16), w_pb[layer].astype(BF16), w_o[layer].astype(BF16),
                  *_split_bf16(jnp.pad(router_w[layer], ((0, 0), (0, LANES - N_EXPERTS)))))
        wg = _to_bf16(w_gate, layer)
        wu = _to_bf16(w_up, layer)
        wd = _to_bf16(w_down, layer)

        wc = w_main[:, :N_XBC] if last else w_main
        res_c = _in_proj(hc, sc1_c, sh1_c, norm1_w[layer], wc, w_dt, bsz * n_ctx, res=pending_c)
        if pending_c is not None:
            p_c, dt_c, hc = res_c
        else:
            p_c, dt_c = res_c
        p_c3 = p_c.reshape(bsz, n_ctx, -1)
        xbc_c = _conv(p_c3, conv_w[layer], conv_b[layer])
        yf_c, yb_c, st_f, st_b = _ssd(xbc_c, dt_c.reshape(bsz, n_ctx, LANES),
                                      dt_bias[layer], a_log[layer], zero_state, zero_state)

        res_l = _in_proj(h, sc1_l, sh1_l, norm1_w[layer], w_main, w_dt, n_lat, res=pending_l)
        if pending_l is not None:
            p_l, dt_l, h = res_l
        else:
            p_l, dt_l = res_l
        p_l3 = p_l.reshape(bsz, n_lat, N_MAIN)
        xbc_l = _conv(p_l3, conv_w[layer], conv_b[layer])
        yf_l, yb_l, _, _ = _ssd(xbc_l, dt_l.reshape(bsz, n_lat, LANES),
                                dt_bias[layer], a_log[layer], st_f, st_b)
        h3, xf_l, aff_l = _mixer(yf_l, yb_l, xbc_l, p_l3, h.reshape(bsz, n_lat, D),
                                 g1_l, sc2_l, sh2_l, consts)
        moe_l = _moe(xf_l, aff_l, _topk(aff_l), wg, wu, wd, bg=1)
        h = h3.reshape(bsz * n_lat, D)
        pending_l = (moe_l.reshape(bsz * n_lat, D), g2_l)

        if not last:
            hc3, xf_c, aff_c = _mixer(yf_c, yb_c, xbc_c, p_c3, hc.reshape(bsz, n_ctx, D),
                                      g1_c, sc2_c, sh2_c, consts)
            moe_c = _moe(xf_c, aff_c, _topk(aff_c), wg, wu, wd, bg=min(8, bsz))
            hc = hc3.reshape(bsz * n_ctx, D)
            pending_c = (moe_c.reshape(bsz * n_ctx, D), g2_c)

    moe_l, g2_l = pending_l
    return _final(h.reshape(bsz, n_lat, D), moe_l.reshape(bsz, n_lat, D), g2_l, final_norm_w)
```

```python
import functools

import jax
import jax.numpy as jnp
from jax import lax
from jax.experimental import pallas as pl
from jax.experimental.pallas import tpu as pltpu

F32 = jnp.float32
BF16 = jnp.bfloat16
HIGHEST = lax.Precision.HIGHEST

D = 1024
CHUNK = 128
EPS = 1e-6
HEADS = 16
HEAD_DIM = 64
GROUPS = 4
STATE = 128
N_XBC = 2048
N_DT = 2 * HEADS
N_SCAN = N_XBC + N_DT
N_REST = 5 * D
MLP_GROUPS = 8
N_EXPERTS = 16
CAPACITY_FACTOR = 2
LANES = 128
VMEM_LIMIT = 56 * 1024 * 1024


def _cparams(*sem):
    return pltpu.CompilerParams(dimension_semantics=sem, vmem_limit_bytes=VMEM_LIMIT)


def _softplus(x):
    return jnp.maximum(x, 0.0) + jnp.log1p(jnp.exp(-jnp.abs(x)))


def _rms(x, w):
    return x * lax.rsqrt(jnp.mean(x * x, axis=-1, keepdims=True) + EPS) * w


def _cast_kernel(w_ref, o_ref):
    o_ref[...] = w_ref[0].astype(o_ref.dtype)


def _to_bf16(w, layer):
    g = w.shape[1]
    return pl.pallas_call(
        _cast_kernel,
        grid=(g, 2),
        in_specs=[pl.BlockSpec((1, 1, D // 2, D), lambda i, j: (layer, i, j, 0))],
        out_specs=pl.BlockSpec((1, D // 2, D), lambda i, j: (i, j, 0)),
        out_shape=jax.ShapeDtypeStruct(w.shape[1:], BF16),
        compiler_params=_cparams("arbitrary", "arbitrary"),
        name="to_bf16",
    )(w)


def _ada_kernel(c_ref, w_ref, b_ref, o_ref):
    a = jax.nn.silu(c_ref[...]).astype(BF16)
    o_ref[0] = jnp.dot(a, w_ref[0].astype(BF16), preferred_element_type=F32) + b_ref[0]


def _ada(cc, ada_w, ada_b):
    n_layers = ada_w.shape[0]
    rows = cc.shape[0]
    return pl.pallas_call(
        _ada_kernel,
        grid=(n_layers, 6),
        in_specs=[pl.BlockSpec((rows, D), lambda l, j: (0, 0)),
                  pl.BlockSpec((1, D, D), lambda l, j: (l, 0, j)),
                  pl.BlockSpec((1, 1, D), lambda l, j: (l, 0, j))],
        out_specs=pl.BlockSpec((1, rows, D), lambda l, j: (l, 0, j)),
        out_shape=jax.ShapeDtypeStruct((n_layers, rows, 6 * D), F32),
        compiler_params=_cparams("arbitrary", "arbitrary"),
        name="ada",
    )(cc, ada_w, ada_b.reshape(n_layers, 1, 6 * D))


HALO = 8


def _in_proj_kernel(seq_len, hp_ref, h_ref, hn_ref, sc_ref, sh_ref, nw_ref, w_ref, wdt_ref,
                    cw_ref, cb_ref, xbc_ref, dt_ref, *rest):
    tm = h_ref.shape[0]
    norm = lambda h: (_rms(h, nw_ref[...]) * (1.0 + sc_ref[0]) + sh_ref[0]).astype(BF16)
    xm = norm(h_ref[...])
    xe = norm(jnp.concatenate([hp_ref[...], hn_ref[...]], axis=0))
    dt_ref[...] = jnp.dot(xm, wdt_ref[...], preferred_element_type=F32)

    row = lax.broadcasted_iota(jnp.int32, (tm, D), 0)
    if seq_len >= tm:
        pos = (pl.program_id(0) % (seq_len // tm)) * tm + row
        first = pos == 0
        last = pos == seq_len - 1
    else:
        first = functools.reduce(jnp.logical_or,
                                 [row == k * seq_len for k in range(tm // seq_len)])
        last = functools.reduce(jnp.logical_or,
                                [row == (k + 1) * seq_len - 1 for k in range(tm // seq_len)])
    def conv_cols(j):
        cols = slice(j * D, (j + 1) * D)
        pm = jnp.dot(xm, w_ref[:, cols], preferred_element_type=F32)
        pe = jnp.dot(xe, w_ref[:, cols], preferred_element_type=F32)
        win = rest[-1].at[j]
        win[0:HALO, :] = pe[:HALO]
        win[HALO:HALO + tm, :] = pm
        win[HALO + tm:, :] = pe[HALO:]
        prev = win[HALO - 1:HALO - 1 + tm, :]
        nxt = win[HALO + 1:HALO + 1 + tm, :]
        y = (jnp.where(first, 0.0, prev) * cw_ref[0:1, cols] + pm * cw_ref[1:2, cols]
             + jnp.where(last, 0.0, nxt) * cw_ref[2:3, cols] + cb_ref[:, cols])
        xbc_ref[:, cols] = jax.nn.silu(y).astype(BF16)

    def rest_cols(j):
        cols = slice(N_XBC + j * D, N_XBC + (j + 1) * D)
        rest[0][:, j * D:(j + 1) * D] = jnp.dot(
            xm, w_ref[:, cols], preferred_element_type=F32).astype(BF16)

    n_rest = rest[0].shape[1] // D if len(rest) > 1 else 0
    order = [("c", 0)] + [("r", j) for j in range(n_rest // 2)] + [("c", 1)] \
        + [("r", j) for j in range(n_rest // 2, n_rest)]
    for kind, j in order:
        (conv_cols if kind == "c" else rest_cols)(j)


def _in_proj(h2, sc, sh, norm_w, w_main, w_dt, conv_w, conv_b, seq_len, tokens_per_mod):
    m = h2.shape[0]
    n_rest = w_main.shape[1] - N_XBC
    tm = min(512, tokens_per_mod)
    tiles_per_mod = tokens_per_mod // tm
    hb = tm // HALO
    row = lambda width: pl.BlockSpec((tm, width), lambda i: (i, 0))
    mod = pl.BlockSpec((1, 1, D), lambda i: (i // tiles_per_mod, 0, 0))
    resident = lambda shape: pl.BlockSpec(shape, lambda i: (0, 0), pipeline_mode=pl.Buffered(1))
    out_specs = [row(N_XBC), row(LANES)]
    out_shape = [jax.ShapeDtypeStruct((m, N_XBC), BF16), jax.ShapeDtypeStruct((m, LANES), F32)]
    if n_rest:
        out_specs.append(row(n_rest))
        out_shape.append(jax.ShapeDtypeStruct((m, n_rest), BF16))
    return pl.pallas_call(
        functools.partial(_in_proj_kernel, seq_len),
        grid=(m // tm,),
        in_specs=[pl.BlockSpec((HALO, D), lambda i: (jnp.maximum(i * hb - 1, 0), 0)),
                  row(D),
                  pl.BlockSpec((HALO, D), lambda i: (jnp.minimum((i + 1) * hb, m // HALO - 1), 0)),
                  mod, mod, resident((1, D)), resident(w_main.shape), resident((D, LANES)),
                  resident((3, N_XBC)), resident((1, N_XBC))],
        out_specs=out_specs, out_shape=out_shape,
        scratch_shapes=[pltpu.VMEM((N_XBC // D, tm + 2 * HALO, D), F32)],
        compiler_params=_cparams("arbitrary"),
        name="in_proj",
    )(h2, h2, h2, sc, sh, norm_w.reshape(1, D), w_main, w_dt, conv_w, conv_b.reshape(1, N_XBC))


def _dtprep_kernel(dt_ref, dtbias_ref, alog_ref, acum_ref, rows_ref):
    nc = rows_ref.shape[1]
    ri = lax.broadcasted_iota(jnp.int32, (CHUNK, CHUNK), 0)
    ci = lax.broadcasted_iota(jnp.int32, (CHUNK, CHUNK), 1)
    tri_f = (ci <= ri).astype(F32)
    tri_b = (ci >= ri).astype(F32)
    fwd_lane = lax.broadcasted_iota(jnp.int32, (CHUNK, LANES), 1) < HEADS
    neg_a = -jnp.exp(alog_ref[...])
    for c in range(nc):
        rs = slice(c * CHUNK, (c + 1) * CHUNK)
        dtv = _softplus(dt_ref[0, rs, :] + dtbias_ref[...])
        a = dtv * neg_a
        acum = jnp.where(
            fwd_lane,
            jnp.dot(tri_f, a, precision=HIGHEST, preferred_element_type=F32),
            jnp.dot(tri_b, a, precision=HIGHEST, preferred_element_type=F32))
        tot = jnp.where(fwd_lane[0:1, :], acum[CHUNK - 1:CHUNK, :], acum[0:1, :])
        w = jnp.exp(tot - acum) * dtv
        acum_ref[0, rs, :] = acum
        rows_ref[0, c, 0:N_DT, :] = acum.T[0:N_DT, :]
        rows_ref[0, c, N_DT:2 * N_DT, :] = dtv.T[0:N_DT, :]
        rows_ref[0, c, 2 * N_DT:3 * N_DT, :] = w.T[0:N_DT, :]


def _dtprep(dt3, dt_bias, a_log):
    b, l, _ = dt3.shape
    nc = l // CHUNK
    pad = lambda v: jnp.pad(v.reshape(1, N_DT), ((0, 0), (0, LANES - N_DT)))
    vec = pl.BlockSpec((1, LANES), lambda i: (0, 0))
    return pl.pallas_call(
        _dtprep_kernel,
        grid=(b,),
        in_specs=[pl.BlockSpec((1, l, LANES), lambda i: (i, 0, 0)), vec, vec],
        out_specs=[pl.BlockSpec((1, l, LANES), lambda i: (i, 0, 0)),
                   pl.BlockSpec((1, nc, 3 * N_DT, CHUNK), lambda i: (i, 0, 0, 0))],
        out_shape=[jax.ShapeDtypeStruct((b, l, LANES), F32),
                   jax.ShapeDtypeStruct((b, nc, 3 * N_DT, CHUNK), F32)],
        compiler_params=_cparams("arbitrary"),
        name="dtprep",
    )(dt3, pad(dt_bias), pad(a_log))


def _ssd_kernel(xf_ref, bf_ref, cf_ref, af_ref, rf_ref, xb_ref, bb_ref, cb_ref, ab_ref, rb_ref,
                h0f_ref, h0b_ref, yf_ref, yb_ref, stf_ref, stb_ref):
    @pl.when(pl.program_id(1) == 0)
    def _():
        stf_ref[...] = h0f_ref[...]
        stb_ref[...] = h0b_ref[...]

    ri = lax.broadcasted_iota(jnp.int32, (CHUNK, CHUNK), 0)
    ci = lax.broadcasted_iota(jnp.int32, (CHUNK, CHUNK), 1)
    lane_lo = ci < HEAD_DIM
    streams = ((xf_ref, bf_ref, cf_ref, af_ref, rf_ref, yf_ref, stf_ref),
               (xb_ref, bb_ref, cb_ref, ab_ref, rb_ref, yb_ref, stb_ref))
    for d, (x_ref, b_ref, c_ref, a_ref, r_ref, y_ref, st_ref) in enumerate(streams):
        mask = (ci <= ri) if d == 0 else (ci >= ri)
        acum = a_ref[0]
        tot = acum[CHUNK - 1:CHUNK, :] if d == 0 else acum[0:1, :]
        cdec = jnp.exp(tot)
        acum_t = r_ref[0, 0, 0:N_DT, :]
        dtv_t = r_ref[0, 0, N_DT:2 * N_DT, :]
        w_t = r_ref[0, 0, 2 * N_DT:3 * N_DT, :]
        for g in range(GROUPS):
            bg = b_ref[0, :, g * STATE:(g + 1) * STATE]
            cg = c_ref[0, :, g * STATE:(g + 1) * STATE]
            cbm = lax.dot_general(cg, bg, (((1,), (1,)), ((), ())),
                                  preferred_element_type=F32)
            bg_t = bg.astype(F32).T
            for q in range(2):
                cols = slice((2 * g + q) * LANES, (2 * g + q + 1) * LANES)
                xp = x_ref[0, :, cols]
                zero = jnp.zeros_like(xp)
                rhs = jnp.concatenate([jnp.where(lane_lo, xp, zero),
                                       jnp.where(lane_lo, zero, xp)], axis=0)
                lane0 = d * HEADS + g * 4 + q * 2
                m_parts, s_parts, e_parts = [], [], []
                for r2 in range(2):
                    ln = lane0 + r2
                    colb = jnp.broadcast_to(acum[:, ln:ln + 1], (CHUNK, CHUNK))
                    seg = colb - acum_t[ln:ln + 1, :]
                    dec = jnp.exp(jnp.where(mask, seg, -jnp.inf))
                    m_parts.append((cbm * dec * dtv_t[ln:ln + 1, :]).astype(BF16))
                    s_parts.append((bg_t * w_t[ln:ln + 1, :]).astype(BF16))
                    e_parts.append(jnp.exp(colb))
                lhs_y = jnp.concatenate(m_parts, axis=1)
                lhs_s = jnp.concatenate(s_parts, axis=1)
                h_t = st_ref[0, g, :, q * LANES:(q + 1) * LANES]
                y_diag = jnp.dot(lhs_y, rhs, preferred_element_type=F32)
                y_off = (jnp.dot(cg, h_t.astype(BF16), preferred_element_type=F32)
                         * jnp.where(lane_lo, e_parts[0], e_parts[1]))
                y_ref[0, :, cols] = (y_diag + y_off).astype(y_ref.dtype)
                cd = jnp.where(lane_lo[0:1, :],
                               jnp.broadcast_to(cdec[:, lane0:lane0 + 1], (1, LANES)),
                               jnp.broadcast_to(cdec[:, lane0 + 1:lane0 + 2], (1, LANES)))
                st_ref[0, g, :, q * LANES:(q + 1) * LANES] = (
                    h_t * cd + jnp.dot(lhs_s, rhs, preferred_element_type=F32))


def _ssd(xbc, dt3, dt_bias, a_log, h0f, h0b):
    b, l, _ = xbc.shape
    nc = l // CHUNK
    acum, rows = _dtprep(dt3, dt_bias, a_log)
    fwd = lambda k: (lambda i, c: (i, c, k))
    bwd = lambda k: (lambda i, c: (i, nc - 1 - c, k))
    xspec = lambda f: pl.BlockSpec((1, CHUNK, D), f(0))
    bspec = lambda f: pl.BlockSpec((1, CHUNK, GROUPS * STATE), f(2))
    cspec = lambda f: pl.BlockSpec((1, CHUNK, GROUPS * STATE), f(3))
    dspec = lambda f: pl.BlockSpec((1, CHUNK, LANES), f(0))
    rspec = lambda rev: pl.BlockSpec(
        (1, 1, 3 * N_DT, CHUNK),
        (lambda i, c: (i, nc - 1 - c, 0, 0)) if rev else (lambda i, c: (i, c, 0, 0)))
    st = pl.BlockSpec((1, GROUPS, STATE, 4 * HEAD_DIM), lambda i, c: (i, 0, 0, 0))
    return pl.pallas_call(
        _ssd_kernel,
        grid=(b, nc),
        in_specs=[xspec(fwd), bspec(fwd), cspec(fwd), dspec(fwd), rspec(False),
                  xspec(bwd), bspec(bwd), cspec(bwd), dspec(bwd), rspec(True),
                  st, st],
        out_specs=[xspec(fwd), xspec(bwd), st, st],
        out_shape=[jax.ShapeDtypeStruct((b, l, D), BF16),
                   jax.ShapeDtypeStruct((b, l, D), BF16),
                   jax.ShapeDtypeStruct(h0f.shape, F32),
                   jax.ShapeDtypeStruct(h0b.shape, F32)],
        compiler_params=_cparams("arbitrary", "arbitrary"),
        name="ssd",
    )(xbc, xbc, xbc, acum, rows, xbc, xbc, xbc, acum, rows, h0f, h0b)


def _split_bf16(x):
    hi = x.astype(BF16)
    return hi, (x - hi.astype(F32)).astype(BF16)


def _mixer_kernel(sub, yf_ref, yb_ref, xs_ref, z_ref, u_ref, v_ref, ga_ref, gb_ref, h_ref,
                  g1_ref, sc2_ref, sh2_ref, dskip_ref, snw_ref, lnw_ref, lnb_ref, n2w_ref,
                  ws_ref, bs_ref, wpa_ref, wpb_ref, wo_ref, rwh_ref, rwl_ref,
                  hout_ref, xf_ref, afft_ref):
    tm = h_ref.shape[1]
    gw = D // GROUPS
    lane = lax.broadcasted_iota(jnp.int32, (sub, LANES), 1)
    for s in range(tm // sub):
        rs = slice(s * sub, (s + 1) * sub)
        y = (yf_ref[0, rs, :].astype(F32) + yb_ref[0, rs, :].astype(F32)
             + dskip_ref[...] * xs_ref[0, rs, :].astype(F32))
        gated = y * jax.nn.silu(z_ref[0, rs, :].astype(F32))
        ssd = jnp.concatenate(
            [_rms(gated[:, k * gw:(k + 1) * gw], snw_ref[:, k * gw:(k + 1) * gw])
             for k in range(GROUPS)], axis=1)
        u = jax.nn.gelu(u_ref[0, rs, :].astype(F32))
        v = jax.nn.gelu(v_ref[0, rs, :].astype(F32))
        mu = jnp.mean(v, axis=-1, keepdims=True)
        var = jnp.mean(jnp.square(v - mu), axis=-1, keepdims=True)
        vn = ((v - mu) * lax.rsqrt(var + EPS) * lnw_ref[...] + lnb_ref[...]).astype(BF16)
        rows = []
        for ck in range(sub // CHUNK):
            cs = slice(ck * CHUNK, (ck + 1) * CHUNK)
            rows.append(jnp.concatenate(
                [jnp.dot(ws_ref[g], vn[cs, g * LANES:(g + 1) * LANES],
                         preferred_element_type=F32)
                 for g in range(MLP_GROUPS)], axis=1) + bs_ref[...])
        mixed = jnp.concatenate(rows, axis=0) if len(rows) > 1 else rows[0]
        sgu = u * mixed
        pa = jnp.dot(ssd.astype(BF16), wpa_ref[...], preferred_element_type=F32)
        pb = jnp.dot(sgu.astype(BF16), wpb_ref[...], preferred_element_type=F32)
        merged = (jax.nn.sigmoid(ga_ref[0, rs, :].astype(F32)) * pa
                  + jax.nn.sigmoid(gb_ref[0, rs, :].astype(F32)) * pb)
        out = jnp.dot(merged.astype(BF16), wo_ref[...], preferred_element_type=F32)
        hn = h_ref[0, rs, :] + g1_ref[0] * out
        hout_ref[0, rs, :] = hn
        xf = _rms(hn, n2w_ref[...]) * (1.0 + sc2_ref[0]) + sh2_ref[0]
        xf_ref[0, rs, :] = xf.astype(BF16)
        xh, xl = _split_bf16(xf)
        logits = (jnp.dot(xh, rwh_ref[...], preferred_element_type=F32)
                  + jnp.dot(xl, rwh_ref[...], preferred_element_type=F32)
                  + jnp.dot(xh, rwl_ref[...], preferred_element_type=F32))
        logits = jnp.where(lane < N_EXPERTS, logits, -jnp.inf)
        ex = jnp.exp(logits - jnp.max(logits, axis=1, keepdims=True))
        aff = ex / jnp.sum(ex, axis=1, keepdims=True)
        afft_ref[0, :, rs] = aff.T[:N_EXPERTS, :]


def _mixer(yf, yb, xbc, p3, h3, g1, sc2, sh2, consts):
    b, l, _ = h3.shape
    sub = min(256, l)
    tm = min(512, l)
    tok = lambda k: pl.BlockSpec((1, tm, D), lambda i, t: (i, t, k))
    mod = pl.BlockSpec((1, 1, D), (lambda i, t: (i, 0, 0)) if g1.shape[0] > 1
                       else (lambda i, t: (0, 0, 0)))
    const = lambda shape: pl.BlockSpec(shape, lambda i, t: (0,) * len(shape),
                                       pipeline_mode=pl.Buffered(1))
    vec = const((1, D))
    wmat = const((D, D))
    return pl.pallas_call(
        functools.partial(_mixer_kernel, sub),
        grid=(b, l // tm),
        in_specs=[tok(0), tok(0), tok(0), tok(0), tok(1), tok(2), tok(3), tok(4), tok(0),
                  mod, mod, mod, vec, vec, vec, vec, vec,
                  const((MLP_GROUPS, CHUNK, CHUNK)), const((CHUNK, D)),
                  wmat, wmat, wmat, const((D, LANES)), const((D, LANES))],
        out_specs=[tok(0), tok(0),
                   pl.BlockSpec((1, N_EXPERTS, tm), lambda i, t: (i, 0, t))],
        out_shape=[jax.ShapeDtypeStruct((b, l, D), F32),
                   jax.ShapeDtypeStruct((b, l, D), BF16),
                   jax.ShapeDtypeStruct((b, N_EXPERTS, l), F32)],
        compiler_params=_cparams("arbitrary", "arbitrary"),
        name="mixer",
    )(yf, yb, xbc, p3, p3, p3, p3, p3, h3, g1, sc2, sh2, *consts)


def _prefix_excl(m):
    n = m.shape[1]
    tri = (lax.broadcasted_iota(jnp.int32, (LANES, LANES), 0)
           <= lax.broadcasted_iota(jnp.int32, (LANES, LANES), 1)).astype(BF16)
    outs = []
    off = jnp.zeros((m.shape[0], 1), F32)
    for k in range(n // LANES):
        mk = m[:, k * LANES:(k + 1) * LANES]
        inc = jnp.dot(mk.astype(BF16), tri, preferred_element_type=F32)
        outs.append(inc - mk + off)
        off = off + inc[:, LANES - 1:LANES]
    return jnp.concatenate(outs, axis=1) if len(outs) > 1 else outs[0]


def _topk_kernel(cap, aff_ref, pos_ref):
    a = aff_ref[...]
    bits = lax.bitcast_convert_type(a, jnp.int32)
    n_exp = a.shape[0]
    count = lambda pred: jnp.sum(jnp.where(pred, 1.0, 0.0), axis=1, keepdims=True)

    def body(_, carry):
        lo, hi = carry
        mid = lo + ((hi - lo + 1) >> 1)
        ok = count(bits >= mid) >= cap
        return jnp.where(ok, mid, lo), jnp.where(ok, hi, mid - 1)

    lo0 = jnp.zeros((n_exp, 1), jnp.int32)
    hi0 = jnp.full((n_exp, 1), 0x7F800000, jnp.int32)
    thr, _ = lax.fori_loop(0, 31, body, (lo0, hi0))
    gt = bits > thr
    eq = bits == thr
    need = cap - count(gt)
    eq_rank = _prefix_excl(jnp.where(eq, 1.0, 0.0))
    sel = gt | (eq & (eq_rank < need))
    slot = _prefix_excl(jnp.where(sel, 1.0, 0.0))
    pos_ref[...] = jnp.where(sel, slot, -1.0).astype(jnp.int32)


def _topk(aff_t):
    b, n_exp, n = aff_t.shape
    cap = CAPACITY_FACTOR * n // N_EXPERTS
    rows = n_exp * min(4, b)
    spec = pl.BlockSpec((rows, n), lambda i: (i, 0))
    pos = pl.pallas_call(
        functools.partial(_topk_kernel, cap),
        grid=(b * n_exp // rows,),
        in_specs=[spec], out_specs=spec,
        out_shape=jax.ShapeDtypeStruct((b * n_exp, n), jnp.int32),
        compiler_params=_cparams("arbitrary"),
        name="topk",
    )(aff_t.reshape(b * n_exp, n))
    return pos.reshape(b, n_exp, n)


def _moe_kernel(cap, final, xf_ref, aff_ref, pos_ref, h_ref, g_ref, fw_ref,
                wg_ref, wu_ref, wd_ref, o_ref):
    bg, n, _ = xf_ref.shape
    e = pl.program_id(1)

    @pl.when(e == 0)
    def _():
        o_ref[...] = jnp.zeros_like(o_ref)

    slot = lax.broadcasted_iota(jnp.int32, (cap, n), 0)
    onehots, gates, parts = [], [], []
    for k in range(bg):
        hit = pos_ref[k, pl.ds(e, 1), :] == slot
        onehot = hit.astype(BF16)
        onehots.append(onehot)
        parts.append(jnp.dot(onehot, xf_ref[k], preferred_element_type=F32).astype(BF16))
        gates.append(jnp.sum(jnp.where(hit, aff_ref[k, pl.ds(e, 1), :], 0.0),
                             axis=1, keepdims=True))
    xg = jnp.concatenate(parts, axis=0) if bg > 1 else parts[0]
    hid = (jax.nn.silu(jnp.dot(xg, wg_ref[0], preferred_element_type=F32))
           * jnp.dot(xg, wu_ref[0], preferred_element_type=F32))
    y = jnp.dot(hid.astype(BF16), wd_ref[0], preferred_element_type=F32)
    for k in range(bg):
        yk = (y[k * cap:(k + 1) * cap] * gates[k]).astype(BF16)
        o_ref[k] += lax.dot_general(onehots[k], yk, (((0,), (0,)), ((), ())),
                                    preferred_element_type=F32)

    @pl.when(e == pl.num_programs(1) - 1)
    def _():
        for k in range(bg):
            hk = h_ref[k] + g_ref[0] * o_ref[k]
            o_ref[k] = _rms(hk, fw_ref[...]) if final else hk


def _moe(xf, aff_t, pos, h3, g2, wg, wu, wd, bg, final_w=None):
    b, n, _ = xf.shape
    cap = CAPACITY_FACTOR * n // N_EXPERTS
    tokens = pl.BlockSpec((bg, n, D), lambda i, e: (i, 0, 0))
    routing = pl.BlockSpec((bg, N_EXPERTS, n), lambda i, e: (i, 0, 0))
    wspec = pl.BlockSpec((1, D, D), lambda i, e: (e, 0, 0))
    final = final_w is not None
    fw = (final_w if final else jnp.ones((D,), F32)).reshape(1, D)
    return pl.pallas_call(
        functools.partial(_moe_kernel, cap, final),
        grid=(b // bg, N_EXPERTS),
        in_specs=[tokens, routing, routing,
                  pl.BlockSpec((bg, n, D), lambda i, e: (i, 0, 0), pipeline_mode=pl.Buffered(1)),
                  pl.BlockSpec((1, 1, D), (lambda i, e: (i, 0, 0)) if g2.shape[0] > 1
                               else (lambda i, e: (0, 0, 0))),
                  pl.BlockSpec((1, D), lambda i, e: (0, 0)),
                  wspec, wspec, wspec],
        out_specs=tokens,
        out_shape=jax.ShapeDtypeStruct((b, n, D), F32),
        compiler_params=_cparams("arbitrary", "arbitrary"),
        name="moe",
    )(xf, aff_t, pos, h3, g2, fw, wg, wu, wd)


def kernel(x, c, ctx, c_ctx, ada_w, ada_b, norm1_w, norm2_w, w_in, conv_w, conv_b, dt_bias, a_log,
           d_skip, ssd_norm_w, sgu_ln_w, sgu_ln_b, w_s, b_s, w_pa, w_pb, w_o, router_w,
           w_gate, w_up, w_down, final_norm_w):
    bsz, n_lat, _ = x.shape
    n_ctx = ctx.shape[1]
    depth = ada_w.shape[0]

    mod_rows = ((bsz + 1 + 7) // 8) * 8
    cc = jnp.zeros((mod_rows, D), F32).at[:bsz].set(c).at[bsz].set(c_ctx)
    mods = _ada(cc, ada_w, ada_b)

    zero_state = jnp.zeros((bsz, GROUPS, STATE, 4 * HEAD_DIM), F32)
    h = x
    hc = ctx
    for layer in range(depth):
        last = layer == depth - 1
        ml = [mods[layer, :bsz, k * D:(k + 1) * D].reshape(bsz, 1, D) for k in range(6)]
        mc = [mods[layer, bsz:bsz + 1, k * D:(k + 1) * D].reshape(1, 1, D) for k in range(6)]
        sh1_l, sc1_l, g1_l, sh2_l, sc2_l, g2_l = ml
        sh1_c, sc1_c, g1_c, sh2_c, sc2_c, g2_c = mc

        w_l = w_in[layer]
        w_main = jnp.concatenate([w_l[:, :N_XBC], w_l[:, N_SCAN:]], axis=1).astype(BF16)
        w_dt = jnp.pad(w_l[:, N_XBC:N_SCAN], ((0, 0), (0, LANES - N_DT))).astype(BF16)
        consts = (jnp.repeat(d_skip[layer], HEAD_DIM).reshape(1, D),
                  ssd_norm_w[layer].reshape(1, D),
                  sgu_ln_w[layer].reshape(1, D), sgu_ln_b[layer].reshape(1, D),
                  norm2_w[layer].reshape(1, D),
                  w_s[layer].astype(BF16),
                  jnp.repeat(b_s[layer].T, LANES, axis=1),
                  w_pa[layer].astype(BF16), w_pb[layer].astype(BF16), w_o[layer].astype(BF16),
                  *_split_bf16(jnp.pad(router_w[layer], ((0, 0), (0, LANES - N_EXPERTS)))))
        wg = _to_bf16(w_gate, layer)
        wu = _to_bf16(w_up, layer)
        wd = _to_bf16(w_down, layer)
        proj = functools.partial(_in_proj, norm_w=norm1_w[layer], w_dt=w_dt,
                                 conv_w=conv_w[layer], conv_b=conv_b[layer])
        scan = functools.partial(_ssd, dt_bias=dt_bias[layer], a_log=a_log[layer])

        out_c = proj(hc.reshape(bsz * n_ctx, D), sc1_c, sh1_c,
                     w_main=w_main[:, :N_XBC] if last else w_main,
                     seq_len=n_ctx, tokens_per_mod=bsz * n_ctx)
        xbc_c = out_c[0].reshape(bsz, n_ctx, N_XBC)
        yf_c, yb_c, st_f, st_b = scan(xbc_c, out_c[1].reshape(bsz, n_ctx, LANES),
                                      h0f=zero_state, h0b=zero_state)

        xbc_l, dt_l, p_l = proj(h.reshape(bsz * n_lat, D), sc1_l, sh1_l, w_main=w_main,
                                seq_len=n_lat, tokens_per_mod=n_lat)
        xbc_l = xbc_l.reshape(bsz, n_lat, N_XBC)
        yf_l, yb_l, _, _ = scan(xbc_l, dt_l.reshape(bsz, n_lat, LANES), h0f=st_f, h0b=st_b)
        h, xf_l, aff_l = _mixer(yf_l, yb_l, xbc_l, p_l.reshape(bsz, n_lat, N_REST), h,
                                g1_l, sc2_l, sh2_l, consts)
        h = _moe(xf_l, aff_l, _topk(aff_l), h, g2_l, wg, wu, wd, bg=1,
                 final_w=final_norm_w if last else None)

        if not last:
            hc, xf_c, aff_c = _mixer(yf_c, yb_c, xbc_c, out_c[2].reshape(bsz, n_ctx, N_REST), hc,
                                     g1_c, sc2_c, sh2_c, consts)
            hc = _moe(xf_c, aff_c, _topk(aff_c), hc, g2_c, wg, wu, wd, bg=min(8, bsz))
    return h
```

```python
import functools

import jax
import jax.numpy as jnp
from jax import lax
from jax.experimental import pallas as pl
from jax.experimental.pallas import tpu as pltpu

F32 = jnp.float32
BF16 = jnp.bfloat16
HIGHEST = lax.Precision.HIGHEST

D = 1024
CHUNK = 128
EPS = 1e-6
HEADS = 16
HEAD_DIM = 64
GROUPS = 4
STATE = 128
N_XBC = 2048
N_DT = 2 * HEADS
N_SCAN = N_XBC + N_DT
N_REST = 5 * D
MLP_GROUPS = 8
N_EXPERTS = 16
CAPACITY_FACTOR = 2
LANES = 128
VMEM_LIMIT = 56 * 1024 * 1024


def _cparams(*sem):
    return pltpu.CompilerParams(dimension_semantics=sem, vmem_limit_bytes=VMEM_LIMIT)


def _softplus(x):
    return jnp.maximum(x, 0.0) + jnp.log1p(jnp.exp(-jnp.abs(x)))


def _rms(x, w):
    return x * lax.rsqrt(jnp.mean(x * x, axis=-1, keepdims=True) + EPS) * w


def _cast_kernel(w_ref, o_ref):
    o_ref[...] = w_ref[0].astype(o_ref.dtype)


def _to_bf16(w, layer):
    g = w.shape[1]
    return pl.pallas_call(
        _cast_kernel,
        grid=(g, 2),
        in_specs=[pl.BlockSpec((1, 1, D // 2, D), lambda i, j: (layer, i, j, 0))],
        out_specs=pl.BlockSpec((1, D // 2, D), lambda i, j: (i, j, 0)),
        out_shape=jax.ShapeDtypeStruct(w.shape[1:], BF16),
        compiler_params=_cparams("arbitrary", "arbitrary"),
        name="to_bf16",
    )(w)


def _ada_kernel(c_ref, w_ref, b_ref, o_ref):
    a = jax.nn.silu(c_ref[...]).astype(BF16)
    o_ref[0] = jnp.dot(a, w_ref[0].astype(BF16), preferred_element_type=F32) + b_ref[0]


def _ada(cc, ada_w, ada_b):
    n_layers = ada_w.shape[0]
    rows = cc.shape[0]
    return pl.pallas_call(
        _ada_kernel,
        grid=(n_layers, 6),
        in_specs=[pl.BlockSpec((rows, D), lambda l, j: (0, 0)),
                  pl.BlockSpec((1, D, D), lambda l, j: (l, 0, j)),
                  pl.BlockSpec((1, 1, D), lambda l, j: (l, 0, j))],
        out_specs=pl.BlockSpec((1, rows, D), lambda l, j: (l, 0, j)),
        out_shape=jax.ShapeDtypeStruct((n_layers, rows, 6 * D), F32),
        compiler_params=_cparams("arbitrary", "arbitrary"),
        name="ada",
    )(cc, ada_w, ada_b.reshape(n_layers, 1, 6 * D))


HALO = 8


def _in_proj_kernel(seq_len, hp_ref, h_ref, hn_ref, sc_ref, sh_ref, nw_ref, w_ref, wdt_ref,
                    cw_ref, cb_ref, xbc_ref, dt_ref, *rest):
    tm = h_ref.shape[0]
    norm = lambda h: (_rms(h, nw_ref[...]) * (1.0 + sc_ref[0]) + sh_ref[0]).astype(BF16)
    xm = norm(h_ref[...])
    xe = norm(jnp.concatenate([hp_ref[...], hn_ref[...]], axis=0))
    dt_ref[...] = jnp.dot(xm, wdt_ref[...], preferred_element_type=F32)

    row = lax.broadcasted_iota(jnp.int32, (tm, D), 0)
    if seq_len >= tm:
        pos = (pl.program_id(0) % (seq_len // tm)) * tm + row
        first = pos == 0
        last = pos == seq_len - 1
    else:
        first = functools.reduce(jnp.logical_or,
                                 [row == k * seq_len for k in range(tm // seq_len)])
        last = functools.reduce(jnp.logical_or,
                                [row == (k + 1) * seq_len - 1 for k in range(tm // seq_len)])
    def conv_cols(j):
        cols = slice(j * D, (j + 1) * D)
        pm = jnp.dot(xm, w_ref[:, cols], preferred_element_type=F32)
        pe = jnp.dot(xe, w_ref[:, cols], preferred_element_type=F32)
        win = rest[-1].at[j]
        win[0:HALO, :] = pe[:HALO]
        win[HALO:HALO + tm, :] = pm
        win[HALO + tm:, :] = pe[HALO:]
        prev = win[HALO - 1:HALO - 1 + tm, :]
        nxt = win[HALO + 1:HALO + 1 + tm, :]
        y = (jnp.where(first, 0.0, prev) * cw_ref[0:1, cols] + pm * cw_ref[1:2, cols]
             + jnp.where(last, 0.0, nxt) * cw_ref[2:3, cols] + cb_ref[:, cols])
        xbc_ref[:, cols] = jax.nn.silu(y).astype(BF16)

    def rest_cols(j):
        cols = slice(N_XBC + j * D, N_XBC + (j + 1) * D)
        rest[0][:, j * D:(j + 1) * D] = jnp.dot(
            xm, w_ref[:, cols], preferred_element_type=F32).astype(BF16)

    n_rest = rest[0].shape[1] // D if len(rest) > 1 else 0
    order = [("c", 0)] + [("r", j) for j in range(n_rest // 2)] + [("c", 1)] \
        + [("r", j) for j in range(n_rest // 2, n_rest)]
    for kind, j in order:
        (conv_cols if kind == "c" else rest_cols)(j)


def _in_proj(h2, sc, sh, norm_w, w_main, w_dt, conv_w, conv_b, seq_len, tokens_per_mod):
    m = h2.shape[0]
    n_rest = w_main.shape[1] - N_XBC
    tm = min(512, tokens_per_mod)
    tiles_per_mod = tokens_per_mod // tm
    hb = tm // HALO
    row = lambda width: pl.BlockSpec((tm, width), lambda i: (i, 0))
    mod = pl.BlockSpec((1, 1, D), lambda i: (i // tiles_per_mod, 0, 0))
    resident = lambda shape: pl.BlockSpec(shape, lambda i: (0, 0), pipeline_mode=pl.Buffered(1))
    out_specs = [row(N_XBC), row(LANES)]
    out_shape = [jax.ShapeDtypeStruct((m, N_XBC), BF16), jax.ShapeDtypeStruct((m, LANES), F32)]
    if n_rest:
        out_specs.append(row(n_rest))
        out_shape.append(jax.ShapeDtypeStruct((m, n_rest), BF16))
    return pl.pallas_call(
        functools.partial(_in_proj_kernel, seq_len),
        grid=(m // tm,),
        in_specs=[pl.BlockSpec((HALO, D), lambda i: (jnp.maximum(i * hb - 1, 0), 0)),
                  row(D),
                  pl.BlockSpec((HALO, D), lambda i: (jnp.minimum((i + 1) * hb, m // HALO - 1), 0)),
                  mod, mod, resident((1, D)), resident(w_main.shape), resident((D, LANES)),
                  resident((3, N_XBC)), resident((1, N_XBC))],
        out_specs=out_specs, out_shape=out_shape,
        scratch_shapes=[pltpu.VMEM((N_XBC // D, tm + 2 * HALO, D), F32)],
        compiler_params=_cparams("arbitrary"),
        name="in_proj",
    )(h2, h2, h2, sc, sh, norm_w.reshape(1, D), w_main, w_dt, conv_w, conv_b.reshape(1, N_XBC))


def _dtprep_kernel(dt_ref, dtbias_ref, alog_ref, acum_ref, rows_ref):
    nc = rows_ref.shape[1]
    ri = lax.broadcasted_iota(jnp.int32, (CHUNK, CHUNK), 0)
    ci = lax.broadcasted_iota(jnp.int32, (CHUNK, CHUNK), 1)
    tri_f = (ci <= ri).astype(F32)
    tri_b = (ci >= ri).astype(F32)
    fwd_lane = lax.broadcasted_iota(jnp.int32, (CHUNK, LANES), 1) < HEADS
    neg_a = -jnp.exp(alog_ref[...])
    for c in range(nc):
        rs = slice(c * CHUNK, (c + 1) * CHUNK)
        dtv = _softplus(dt_ref[0, rs, :] + dtbias_ref[...])
        a = dtv * neg_a
        acum = jnp.where(
            fwd_lane,
            jnp.dot(tri_f, a, precision=HIGHEST, preferred_element_type=F32),
            jnp.dot(tri_b, a, precision=HIGHEST, preferred_element_type=F32))
        tot = jnp.where(fwd_lane[0:1, :], acum[CHUNK - 1:CHUNK, :], acum[0:1, :])
        w = jnp.exp(tot - acum) * dtv
        acum_ref[0, rs, :] = acum
        rows_ref[0, c, 0:N_DT, :] = acum.T[0:N_DT, :]
        rows_ref[0, c, N_DT:2 * N_DT, :] = dtv.T[0:N_DT, :]
        rows_ref[0, c, 2 * N_DT:3 * N_DT, :] = w.T[0:N_DT, :]


def _dtprep(dt3, dt_bias, a_log):
    b, l, _ = dt3.shape
    nc = l // CHUNK
    pad = lambda v: jnp.pad(v.reshape(1, N_DT), ((0, 0), (0, LANES - N_DT)))
    vec = pl.BlockSpec((1, LANES), lambda i: (0, 0))
    return pl.pallas_call(
        _dtprep_kernel,
        grid=(b,),
        in_specs=[pl.BlockSpec((1, l, LANES), lambda i: (i, 0, 0)), vec, vec],
        out_specs=[pl.BlockSpec((1, l, LANES), lambda i: (i, 0, 0)),
                   pl.BlockSpec((1, nc, 3 * N_DT, CHUNK), lambda i: (i, 0, 0, 0))],
        out_shape=[jax.ShapeDtypeStruct((b, l, LANES), F32),
                   jax.ShapeDtypeStruct((b, nc, 3 * N_DT, CHUNK), F32)],
        compiler_params=_cparams("arbitrary"),
        name="dtprep",
    )(dt3, pad(dt_bias), pad(a_log))


def _ssd_kernel(xf_ref, bf_ref, cf_ref, af_ref, rf_ref, xb_ref, bb_ref, cb_ref, ab_ref, rb_ref,
                h0f_ref, h0b_ref, yf_ref, yb_ref, stf_ref, stb_ref):
    @pl.when(pl.program_id(1) == 0)
    def _():
        stf_ref[...] = h0f_ref[...]
        stb_ref[...] = h0b_ref[...]

    ri = lax.broadcasted_iota(jnp.int32, (CHUNK, CHUNK), 0)
    ci = lax.broadcasted_iota(jnp.int32, (CHUNK, CHUNK), 1)
    lane_lo = ci < HEAD_DIM
    streams = ((xf_ref, bf_ref, cf_ref, af_ref, rf_ref, yf_ref, stf_ref),
               (xb_ref, bb_ref, cb_ref, ab_ref, rb_ref, yb_ref, stb_ref))
    for d, (x_ref, b_ref, c_ref, a_ref, r_ref, y_ref, st_ref) in enumerate(streams):
        mask = (ci <= ri) if d == 0 else (ci >= ri)
        acum = a_ref[0]
        tot = acum[CHUNK - 1:CHUNK, :] if d == 0 else acum[0:1, :]
        cdec = jnp.exp(tot)
        acum_t = r_ref[0, 0, 0:N_DT, :]
        dtv_t = r_ref[0, 0, N_DT:2 * N_DT, :]
        w_t = r_ref[0, 0, 2 * N_DT:3 * N_DT, :]
        for g in range(GROUPS):
            bg = b_ref[0, :, g * STATE:(g + 1) * STATE]
            cg = c_ref[0, :, g * STATE:(g + 1) * STATE]
            cbm = lax.dot_general(cg, bg, (((1,), (1,)), ((), ())),
                                  preferred_element_type=F32)
            bg_t = bg.astype(F32).T
            for q in range(2):
                cols = slice((2 * g + q) * LANES, (2 * g + q + 1) * LANES)
                xp = x_ref[0, :, cols]
                zero = jnp.zeros_like(xp)
                rhs = jnp.concatenate([jnp.where(lane_lo, xp, zero),
                                       jnp.where(lane_lo, zero, xp)], axis=0)
                lane0 = d * HEADS + g * 4 + q * 2
                m_parts, s_parts, e_parts = [], [], []
                for r2 in range(2):
                    ln = lane0 + r2
                    colb = jnp.broadcast_to(acum[:, ln:ln + 1], (CHUNK, CHUNK))
                    seg = colb - acum_t[ln:ln + 1, :]
                    dec = jnp.exp(jnp.where(mask, seg, -jnp.inf))
                    m_parts.append((cbm * dec * dtv_t[ln:ln + 1, :]).astype(BF16))
                    s_parts.append((bg_t * w_t[ln:ln + 1, :]).astype(BF16))
                    e_parts.append(jnp.exp(colb))
                lhs_y = jnp.concatenate(m_parts, axis=1)
                lhs_s = jnp.concatenate(s_parts, axis=1)
                h_t = st_ref[0, g, :, q * LANES:(q + 1) * LANES]
                y_diag = jnp.dot(lhs_y, rhs, preferred_element_type=F32)
                y_off = (jnp.dot(cg, h_t.astype(BF16), preferred_element_type=F32)
                         * jnp.where(lane_lo, e_parts[0], e_parts[1]))
                y_ref[0, :, cols] = (y_diag + y_off).astype(y_ref.dtype)
                cd = jnp.where(lane_lo[0:1, :],
                               jnp.broadcast_to(cdec[:, lane0:lane0 + 1], (1, LANES)),
                               jnp.broadcast_to(cdec[:, lane0 + 1:lane0 + 2], (1, LANES)))
                st_ref[0, g, :, q * LANES:(q + 1) * LANES] = (
                    h_t * cd + jnp.dot(lhs_s, rhs, preferred_element_type=F32))


def _ssd(xbc, dt3, dt_bias, a_log, h0f, h0b):
    b, l, _ = xbc.shape
    nc = l // CHUNK
    acum, rows = _dtprep(dt3, dt_bias, a_log)
    fwd = lambda k: (lambda i, c: (i, c, k))
    bwd = lambda k: (lambda i, c: (i, nc - 1 - c, k))
    xspec = lambda f: pl.BlockSpec((1, CHUNK, D), f(0))
    bspec = lambda f: pl.BlockSpec((1, CHUNK, GROUPS * STATE), f(2))
    cspec = lambda f: pl.BlockSpec((1, CHUNK, GROUPS * STATE), f(3))
    dspec = lambda f: pl.BlockSpec((1, CHUNK, LANES), f(0))
    rspec = lambda rev: pl.BlockSpec(
        (1, 1, 3 * N_DT, CHUNK),
        (lambda i, c: (i, nc - 1 - c, 0, 0)) if rev else (lambda i, c: (i, c, 0, 0)))
    st = pl.BlockSpec((1, GROUPS, STATE, 4 * HEAD_DIM), lambda i, c: (i, 0, 0, 0))
    return pl.pallas_call(
        _ssd_kernel,
        grid=(b, nc),
        in_specs=[xspec(fwd), bspec(fwd), cspec(fwd), dspec(fwd), rspec(False),
                  xspec(bwd), bspec(bwd), cspec(bwd), dspec(bwd), rspec(True),
                  st, st],
        out_specs=[xspec(fwd), xspec(bwd), st, st],
        out_shape=[jax.ShapeDtypeStruct((b, l, D), BF16),
                   jax.ShapeDtypeStruct((b, l, D), BF16),
                   jax.ShapeDtypeStruct(h0f.shape, F32),
                   jax.ShapeDtypeStruct(h0b.shape, F32)],
        compiler_params=_cparams("arbitrary", "arbitrary"),
        name="ssd",
    )(xbc, xbc, xbc, acum, rows, xbc, xbc, xbc, acum, rows, h0f, h0b)


def _split_bf16(x):
    hi = x.astype(BF16)
    return hi, (x - hi.astype(F32)).astype(BF16)


def _mixer_kernel(sub, yf_ref, yb_ref, xs_ref, z_ref, u_ref, v_ref, ga_ref, gb_ref, h_ref,
                  g1_ref, sc2_ref, sh2_ref, dskip_ref, snw_ref, lnw_ref, lnb_ref, n2w_ref,
                  ws_ref, bs_ref, wpa_ref, wpb_ref, wo_ref, rwh_ref, rwl_ref,
                  hout_ref, xf_ref, afft_ref):
    tm = h_ref.shape[1]
    gw = D // GROUPS
    lane = lax.broadcasted_iota(jnp.int32, (sub, LANES), 1)
    for s in range(tm // sub):
        rs = slice(s * sub, (s + 1) * sub)
        y = (yf_ref[0, rs, :].astype(F32) + yb_ref[0, rs, :].astype(F32)
             + dskip_ref[...] * xs_ref[0, rs, :].astype(F32))
        gated = y * jax.nn.silu(z_ref[0, rs, :].astype(F32))
        ssd = jnp.concatenate(
            [_rms(gated[:, k * gw:(k + 1) * gw], snw_ref[:, k * gw:(k + 1) * gw])
             for k in range(GROUPS)], axis=1)
        u = jax.nn.gelu(u_ref[0, rs, :].astype(F32))
        v = jax.nn.gelu(v_ref[0, rs, :].astype(F32))
        mu = jnp.mean(v, axis=-1, keepdims=True)
        var = jnp.mean(jnp.square(v - mu), axis=-1, keepdims=True)
        vn = ((v - mu) * lax.rsqrt(var + EPS) * lnw_ref[...] + lnb_ref[...]).astype(BF16)
        rows = []
        for ck in range(sub // CHUNK):
            cs = slice(ck * CHUNK, (ck + 1) * CHUNK)
            rows.append(jnp.concatenate(
                [jnp.dot(ws_ref[g], vn[cs, g * LANES:(g + 1) * LANES],
                         preferred_element_type=F32)
                 for g in range(MLP_GROUPS)], axis=1) + bs_ref[...])
        mixed = jnp.concatenate(rows, axis=0) if len(rows) > 1 else rows[0]
        sgu = u * mixed
        pa = jnp.dot(ssd.astype(BF16), wpa_ref[...], preferred_element_type=F32)
        pb = jnp.dot(sgu.astype(BF16), wpb_ref[...], preferred_element_type=F32)
        merged = (jax.nn.sigmoid(ga_ref[0, rs, :].astype(F32)) * pa
                  + jax.nn.sigmoid(gb_ref[0, rs, :].astype(F32)) * pb)
        out = jnp.dot(merged.astype(BF16), wo_ref[...], preferred_element_type=F32)
        hn = h_ref[0, rs, :] + g1_ref[0] * out
        hout_ref[0, rs, :] = hn
        xf = _rms(hn, n2w_ref[...]) * (1.0 + sc2_ref[0]) + sh2_ref[0]
        xf_ref[0, rs, :] = xf.astype(BF16)
        xh, xl = _split_bf16(xf)
        logits = (jnp.dot(xh, rwh_ref[...], preferred_element_type=F32)
                  + jnp.dot(xl, rwh_ref[...], preferred_element_type=F32)
                  + jnp.dot(xh, rwl_ref[...], preferred_element_type=F32))
        logits = jnp.where(lane < N_EXPERTS, logits, -jnp.inf)
        ex = jnp.exp(logits - jnp.max(logits, axis=1, keepdims=True))
        aff = ex / jnp.sum(ex, axis=1, keepdims=True)
        afft_ref[0, :, rs] = aff.T[:N_EXPERTS, :]


def _mixer(yf, yb, xbc, p3, h3, g1, sc2, sh2, consts):
    b, l, _ = h3.shape
    sub = min(256, l)
    tm = min(512, l)
    tok = lambda k: pl.BlockSpec((1, tm, D), lambda i, t: (i, t, k))
    mod = pl.BlockSpec((1, 1, D), (lambda i, t: (i, 0, 0)) if g1.shape[0] > 1
                       else (lambda i, t: (0, 0, 0)))
    const = lambda shape: pl.BlockSpec(shape, lambda i, t: (0,) * len(shape),
                                       pipeline_mode=pl.Buffered(1))
    vec = const((1, D))
    wmat = const((D, D))
    return pl.pallas_call(
        functools.partial(_mixer_kernel, sub),
        grid=(b, l // tm),
        in_specs=[tok(0), tok(0), tok(0), tok(0), tok(1), tok(2), tok(3), tok(4), tok(0),
                  mod, mod, mod, vec, vec, vec, vec, vec,
                  const((MLP_GROUPS, CHUNK, CHUNK)), const((CHUNK, D)),
                  wmat, wmat, wmat, const((D, LANES)), const((D, LANES))],
        out_specs=[tok(0), tok(0),
                   pl.BlockSpec((1, N_EXPERTS, tm), lambda i, t: (i, 0, t))],
        out_shape=[jax.ShapeDtypeStruct((b, l, D), F32),
                   jax.ShapeDtypeStruct((b, l, D), BF16),
                   jax.ShapeDtypeStruct((b, N_EXPERTS, l), F32)],
        compiler_params=_cparams("arbitrary", "arbitrary"),
        name="mixer",
    )(yf, yb, xbc, p3, p3, p3, p3, p3, h3, g1, sc2, sh2, *consts)


def _prefix_excl(m):
    n = m.shape[1]
    tri = (lax.broadcasted_iota(jnp.int32, (LANES, LANES), 0)
           <= lax.broadcasted_iota(jnp.int32, (LANES, LANES), 1)).astype(BF16)
    outs = []
    off = jnp.zeros((m.shape[0], 1), F32)
    for k in range(n // LANES):
        mk = m[:, k * LANES:(k + 1) * LANES]
        inc = jnp.dot(mk.astype(BF16), tri, preferred_element_type=F32)
        outs.append(inc - mk + off)
        off = off + inc[:, LANES - 1:LANES]
    return jnp.concatenate(outs, axis=1) if len(outs) > 1 else outs[0]


def _topk_kernel(cap, aff_ref, pos_ref):
    a = aff_ref[...]
    bits = lax.bitcast_convert_type(a, jnp.int32)
    n_exp = a.shape[0]
    count = lambda pred: jnp.sum(jnp.where(pred, 1.0, 0.0), axis=1, keepdims=True)

    def body(_, carry):
        lo, hi = carry
        mid = lo + ((hi - lo + 1) >> 1)
        ok = count(bits >= mid) >= cap
        return jnp.where(ok, mid, lo), jnp.where(ok, hi, mid - 1)

    lo0 = jnp.zeros((n_exp, 1), jnp.int32)
    hi0 = jnp.full((n_exp, 1), 0x7F800000, jnp.int32)
    thr, _ = lax.fori_loop(0, 31, body, (lo0, hi0))
    gt = bits > thr
    eq = bits == thr
    need = cap - count(gt)
    eq_rank = _prefix_excl(jnp.where(eq, 1.0, 0.0))
    sel = gt | (eq & (eq_rank < need))
    slot = _prefix_excl(jnp.where(sel, 1.0, 0.0))
    pos_ref[...] = jnp.where(sel, slot, -1.0).astype(jnp.int32)


def _topk(aff_t):
    b, n_exp, n = aff_t.shape
    cap = CAPACITY_FACTOR * n // N_EXPERTS
    rows = n_exp * min(4, b)
    spec = pl.BlockSpec((rows, n), lambda i: (i, 0))
    pos = pl.pallas_call(
        functools.partial(_topk_kernel, cap),
        grid=(b * n_exp // rows,),
        in_specs=[spec], out_specs=spec,
        out_shape=jax.ShapeDtypeStruct((b * n_exp, n), jnp.int32),
        compiler_params=_cparams("arbitrary"),
        name="topk",
    )(aff_t.reshape(b * n_exp, n))
    return pos.reshape(b, n_exp, n)


def _moe_kernel(cap, final, xf_ref, aff_ref, pos_ref, h_ref, g_ref, fw_ref,
                wg_ref, wu_ref, wd_ref, o_ref):
    bg, n, _ = xf_ref.shape
    ch = h_ref.shape[1]
    e = pl.program_id(1)

    @pl.when(e == 0)
    def _():
        o_ref[...] = jnp.zeros_like(o_ref)

    @pl.when(e < N_EXPERTS)
    def _():
        slot = lax.broadcasted_iota(jnp.int32, (cap, n), 0)
        onehots, gates, parts = [], [], []
        for k in range(bg):
            hit = pos_ref[k, pl.ds(e, 1), :] == slot
            onehot = hit.astype(BF16)
            onehots.append(onehot)
            parts.append(jnp.dot(onehot, xf_ref[k], preferred_element_type=F32).astype(BF16))
            gates.append(jnp.sum(jnp.where(hit, aff_ref[k, pl.ds(e, 1), :], 0.0),
                                 axis=1, keepdims=True))
        xg = jnp.concatenate(parts, axis=0) if bg > 1 else parts[0]
        hid = (jax.nn.silu(jnp.dot(xg, wg_ref[0], preferred_element_type=F32))
               * jnp.dot(xg, wu_ref[0], preferred_element_type=F32))
        y = jnp.dot(hid.astype(BF16), wd_ref[0], preferred_element_type=F32)
        for k in range(bg):
            yk = (y[k * cap:(k + 1) * cap] * gates[k]).astype(BF16)
            o_ref[k] += lax.dot_general(onehots[k], yk, (((0,), (0,)), ((), ())),
                                        preferred_element_type=F32)

    @pl.when(e >= N_EXPERTS)
    def _():
        rows = pl.ds(pl.multiple_of((e - N_EXPERTS) * ch, ch), ch)
        for k in range(bg):
            hk = h_ref[k] + g_ref[0] * o_ref[k, rows, :]
            o_ref[k, rows, :] = _rms(hk, fw_ref[...]) if final else hk


def _moe(xf, aff_t, pos, h3, g2, wg, wu, wd, bg, final_w=None):
    b, n, _ = xf.shape
    cap = CAPACITY_FACTOR * n // N_EXPERTS
    ch = min(512, n)
    last_e = N_EXPERTS - 1
    tokens = pl.BlockSpec((bg, n, D), lambda i, e: (i, 0, 0))
    routing = pl.BlockSpec((bg, N_EXPERTS, n), lambda i, e: (i, 0, 0))
    wspec = pl.BlockSpec((1, D, D), lambda i, e: (jnp.minimum(e, last_e), 0, 0))
    final = final_w is not None
    fw = (final_w if final else jnp.ones((D,), F32)).reshape(1, D)
    return pl.pallas_call(
        functools.partial(_moe_kernel, cap, final),
        grid=(b // bg, N_EXPERTS + n // ch),
        in_specs=[tokens, routing, routing,
                  pl.BlockSpec((bg, ch, D), lambda i, e: (i, jnp.maximum(e - N_EXPERTS, 0), 0)),
                  pl.BlockSpec((1, 1, D), (lambda i, e: (i, 0, 0)) if g2.shape[0] > 1
                               else (lambda i, e: (0, 0, 0))),
                  pl.BlockSpec((1, D), lambda i, e: (0, 0)),
                  wspec, wspec, wspec],
        out_specs=tokens,
        out_shape=jax.ShapeDtypeStruct((b, n, D), F32),
        compiler_params=_cparams("arbitrary", "arbitrary"),
        name="moe",
    )(xf, aff_t, pos, h3, g2, fw, wg, wu, wd)


def kernel(x, c, ctx, c_ctx, ada_w, ada_b, norm1_w, norm2_w, w_in, conv_w, conv_b, dt_bias, a_log,
           d_skip, ssd_norm_w, sgu_ln_w, sgu_ln_b, w_s, b_s, w_pa, w_pb, w_o, router_w,
           w_gate, w_up, w_down, final_norm_w):
    bsz, n_lat, _ = x.shape
    n_ctx = ctx.shape[1]
    depth = ada_w.shape[0]

    mod_rows = ((bsz + 1 + 7) // 8) * 8
    cc = jnp.zeros((mod_rows, D), F32).at[:bsz].set(c).at[bsz].set(c_ctx)
    mods = _ada(cc, ada_w, ada_b)

    zero_state = jnp.zeros((bsz, GROUPS, STATE, 4 * HEAD_DIM), F32)
    h = x
    hc = ctx
    for layer in range(depth):
        last = layer == depth - 1
        ml = [mods[layer, :bsz, k * D:(k + 1) * D].reshape(bsz, 1, D) for k in range(6)]
        mc = [mods[layer, bsz:bsz + 1, k * D:(k + 1) * D].reshape(1, 1, D) for k in range(6)]
        sh1_l, sc1_l, g1_l, sh2_l, sc2_l, g2_l = ml
        sh1_c, sc1_c, g1_c, sh2_c, sc2_c, g2_c = mc

        w_l = w_in[layer]
        w_main = jnp.concatenate([w_l[:, :N_XBC], w_l[:, N_SCAN:]], axis=1).astype(BF16)
        w_dt = jnp.pad(w_l[:, N_XBC:N_SCAN], ((0, 0), (0, LANES - N_DT))).astype(BF16)
        consts = (jnp.repeat(d_skip[layer], HEAD_DIM).reshape(1, D),
                  ssd_norm_w[layer].reshape(1, D),
                  sgu_ln_w[layer].reshape(1, D), sgu_ln_b[layer].reshape(1, D),
                  norm2_w[layer].reshape(1, D),
                  w_s[layer].astype(BF16),
                  jnp.repeat(b_s[layer].T, LANES, axis=1),
                  w_pa[layer].astype(BF16), w_pb[layer].astype(BF16), w_o[layer].astype(BF16),
                  *_split_bf16(jnp.pad(router_w[layer], ((0, 0), (0, LANES - N_EXPERTS)))))
        wg = _to_bf16(w_gate, layer)
        wu = _to_bf16(w_up, layer)
        wd = _to_bf16(w_down, layer)
        proj = functools.partial(_in_proj, norm_w=norm1_w[layer], w_dt=w_dt,
                                 conv_w=conv_w[layer], conv_b=conv_b[layer])
        scan = functools.partial(_ssd, dt_bias=dt_bias[layer], a_log=a_log[layer])

        out_c = proj(hc.reshape(bsz * n_ctx, D), sc1_c, sh1_c,
                     w_main=w_main[:, :N_XBC] if last else w_main,
                     seq_len=n_ctx, tokens_per_mod=bsz * n_ctx)
        xbc_c = out_c[0].reshape(bsz, n_ctx, N_XBC)
        yf_c, yb_c, st_f, st_b = scan(xbc_c, out_c[1].reshape(bsz, n_ctx, LANES),
                                      h0f=zero_state, h0b=zero_state)

        xbc_l, dt_l, p_l = proj(h.reshape(bsz * n_lat, D), sc1_l, sh1_l, w_main=w_main,
                                seq_len=n_lat, tokens_per_mod=n_lat)
        xbc_l = xbc_l.reshape(bsz, n_lat, N_XBC)
        yf_l, yb_l, _, _ = scan(xbc_l, dt_l.reshape(bsz, n_lat, LANES), h0f=st_f, h0b=st_b)
        h, xf_l, aff_l = _mixer(yf_l, yb_l, xbc_l, p_l.reshape(bsz, n_lat, N_REST), h,
                                g1_l, sc2_l, sh2_l, consts)
        h = _moe(xf_l, aff_l, _topk(aff_l), h, g2_l, wg, wu, wd, bg=1,
                 final_w=final_norm_w if last else None)

        if not last:
            hc, xf_c, aff_c = _mixer(yf_c, yb_c, xbc_c, out_c[2].reshape(bsz, n_ctx, N_REST), hc,
                                     g1_c, sc2_c, sh2_c, consts)
            hc = _moe(xf_c, aff_c, _topk(aff_c), hc, g2_c, wg, wu, wd, bg=min(8, bsz))
    return h
```

```python
import functools

import jax
import jax.numpy as jnp
from jax import lax
from jax.experimental import pallas as pl
from jax.experimental.pallas import tpu as pltpu

F32 = jnp.float32
BF16 = jnp.bfloat16
HIGHEST = lax.Precision.HIGHEST

D = 1024
CHUNK = 128
EPS = 1e-6
HEADS = 16
HEAD_DIM = 64
GROUPS = 4
STATE = 128
N_XBC = 2048
N_DT = 2 * HEADS
N_SCAN = N_XBC + N_DT
N_REST = 5 * D
MLP_GROUPS = 8
N_EXPERTS = 16
CAPACITY_FACTOR = 2
LANES = 128
VMEM_LIMIT = 56 * 1024 * 1024


def _cparams(*sem):
    return pltpu.CompilerParams(dimension_semantics=sem, vmem_limit_bytes=VMEM_LIMIT)


def _softplus(x):
    return jnp.maximum(x, 0.0) + jnp.log1p(jnp.exp(-jnp.abs(x)))


def _rms(x, w):
    return x * lax.rsqrt(jnp.mean(x * x, axis=-1, keepdims=True) + EPS) * w


def _cast_kernel(w_ref, o_ref):
    o_ref[...] = w_ref[0].astype(o_ref.dtype)


def _to_bf16(w, layer):
    g = w.shape[1]
    return pl.pallas_call(
        _cast_kernel,
        grid=(g, 2),
        in_specs=[pl.BlockSpec((1, 1, D // 2, D), lambda i, j: (layer, i, j, 0))],
        out_specs=pl.BlockSpec((1, D // 2, D), lambda i, j: (i, j, 0)),
        out_shape=jax.ShapeDtypeStruct(w.shape[1:], BF16),
        compiler_params=_cparams("arbitrary", "arbitrary"),
        name="to_bf16",
    )(w)


def _ada_kernel(c_ref, w_ref, b_ref, o_ref):
    a = jax.nn.silu(c_ref[...]).astype(BF16)
    o_ref[0] = jnp.dot(a, w_ref[0].astype(BF16), preferred_element_type=F32) + b_ref[0]


def _ada(cc, ada_w, ada_b):
    n_layers = ada_w.shape[0]
    rows = cc.shape[0]
    return pl.pallas_call(
        _ada_kernel,
        grid=(n_layers, 6),
        in_specs=[pl.BlockSpec((rows, D), lambda l, j: (0, 0)),
                  pl.BlockSpec((1, D, D), lambda l, j: (l, 0, j)),
                  pl.BlockSpec((1, 1, D), lambda l, j: (l, 0, j))],
        out_specs=pl.BlockSpec((1, rows, D), lambda l, j: (l, 0, j)),
        out_shape=jax.ShapeDtypeStruct((n_layers, rows, 6 * D), F32),
        compiler_params=_cparams("arbitrary", "arbitrary"),
        name="ada",
    )(cc, ada_w, ada_b.reshape(n_layers, 1, 6 * D))


HALO = 16
CONV_SUB = 256


def _in_proj_kernel(seq_len, with_res, *refs):
    if with_res:
        (hp_ref, h_ref, hn_ref, mp_ref, m_ref, mn_ref, g_ref, sc_ref, sh_ref, nw_ref, w_ref,
         wdt_ref, cw_ref, cb_ref, xbc_ref, dt_ref, hout_ref, *rest) = refs
        h = h_ref[...] + g_ref[0] * m_ref[...]
        hout_ref[...] = h
        he = (jnp.concatenate([hp_ref[...], hn_ref[...]], axis=0)
              + g_ref[0] * jnp.concatenate([mp_ref[...], mn_ref[...]], axis=0))
    else:
        (hp_ref, h_ref, hn_ref, sc_ref, sh_ref, nw_ref, w_ref,
         wdt_ref, cw_ref, cb_ref, xbc_ref, dt_ref, *rest) = refs
        h = h_ref[...]
        he = jnp.concatenate([hp_ref[...], hn_ref[...]], axis=0)
    tm = h.shape[0]
    sub = min(CONV_SUB, tm)
    norm = lambda v: (_rms(v, nw_ref[...]) * (1.0 + sc_ref[0]) + sh_ref[0]).astype(BF16)
    xm = norm(h)
    xe = norm(he)
    xext = jnp.concatenate([xe[:HALO], xm, xe[HALO:]], axis=0)
    n_rest = rest[0].shape[1] // D if rest else 0
    row = lax.broadcasted_iota(jnp.int32, (sub, D), 0)
    for s in range(tm // sub):
        rs = slice(s * sub, (s + 1) * sub)
        if seq_len >= tm:
            pos = (pl.program_id(0) % (seq_len // tm)) * tm + s * sub + row
            first = pos == 0
            last = pos == seq_len - 1
        else:
            first = functools.reduce(jnp.logical_or, [row + s * sub == k * seq_len
                                                      for k in range(tm // seq_len)])
            last = functools.reduce(jnp.logical_or, [row + s * sub == (k + 1) * seq_len - 1
                                                     for k in range(tm // seq_len)])
        xs = xm[rs]
        xs_ext = xext[s * sub:(s + 1) * sub + 2 * HALO]
        dt_ref[rs, :] = jnp.dot(xs, wdt_ref[...], preferred_element_type=F32)
        for j in range(N_XBC // D):
            cols = slice(j * D, (j + 1) * D)
            pf = jnp.dot(xs_ext, w_ref[:, cols], preferred_element_type=F32)
            prev = pltpu.roll(pf, 1, 0)[HALO:HALO + sub]
            nxt = pltpu.roll(pf, sub + 2 * HALO - 1, 0)[HALO:HALO + sub]
            y = (jnp.where(first, 0.0, prev) * cw_ref[0:1, cols]
                 + pf[HALO:HALO + sub] * cw_ref[1:2, cols]
                 + jnp.where(last, 0.0, nxt) * cw_ref[2:3, cols] + cb_ref[:, cols])
            xbc_ref[rs, cols] = jax.nn.silu(y).astype(BF16)
        for j in range(n_rest):
            cols = slice(N_XBC + j * D, N_XBC + (j + 1) * D)
            rest[0][rs, j * D:(j + 1) * D] = jnp.dot(
                xs, w_ref[:, cols], preferred_element_type=F32).astype(BF16)


def _in_proj(h2, sc, sh, norm_w, w_main, w_dt, conv_w, conv_b, seq_len, tokens_per_mod, res=None):
    m = h2.shape[0]
    n_rest = w_main.shape[1] - N_XBC
    tm = min(512, tokens_per_mod)
    tiles_per_mod = tokens_per_mod // tm
    hb = tm // HALO
    with_res = res is not None
    row = lambda width: pl.BlockSpec((tm, width), lambda i: (i, 0))
    halo_prev = pl.BlockSpec((HALO, D), lambda i: (jnp.maximum(i * hb - 1, 0), 0))
    halo_next = pl.BlockSpec((HALO, D), lambda i: (jnp.minimum((i + 1) * hb, m // HALO - 1), 0))
    mod = pl.BlockSpec((1, 1, D), lambda i: (i // tiles_per_mod, 0, 0))
    resident = lambda shape: pl.BlockSpec(shape, lambda i: (0, 0), pipeline_mode=pl.Buffered(1))
    in_specs = [halo_prev, row(D), halo_next]
    args = [h2, h2, h2]
    if with_res:
        in_specs += [halo_prev, row(D), halo_next, mod]
        args += [res[0], res[0], res[0], res[1]]
    in_specs += [mod, mod, resident((1, D)), resident(w_main.shape), resident((D, LANES)),
                 resident((3, N_XBC)), resident((1, N_XBC))]
    args += [sc, sh, norm_w.reshape(1, D), w_main, w_dt, conv_w, conv_b.reshape(1, N_XBC)]
    out_specs = [row(N_XBC), row(LANES)]
    out_shape = [jax.ShapeDtypeStruct((m, N_XBC), BF16), jax.ShapeDtypeStruct((m, LANES), F32)]
    if with_res:
        out_specs.append(row(D))
        out_shape.append(jax.ShapeDtypeStruct((m, D), F32))
    if n_rest:
        out_specs.append(row(n_rest))
        out_shape.append(jax.ShapeDtypeStruct((m, n_rest), BF16))
    return pl.pallas_call(
        functools.partial(_in_proj_kernel, seq_len, with_res),
        grid=(m // tm,),
        in_specs=in_specs, out_specs=out_specs, out_shape=out_shape,
        compiler_params=_cparams("arbitrary"),
        name="in_proj_res" if with_res else "in_proj",
    )(*args)


def _dtprep_kernel(dt_ref, dtbias_ref, alog_ref, acum_ref, rows_ref):
    nc = rows_ref.shape[1]
    ri = lax.broadcasted_iota(jnp.int32, (CHUNK, CHUNK), 0)
    ci = lax.broadcasted_iota(jnp.int32, (CHUNK, CHUNK), 1)
    tri_f = (ci <= ri).astype(F32)
    tri_b = (ci >= ri).astype(F32)
    fwd_lane = lax.broadcasted_iota(jnp.int32, (CHUNK, LANES), 1) < HEADS
    neg_a = -jnp.exp(alog_ref[...])
    for c in range(nc):
        rs = slice(c * CHUNK, (c + 1) * CHUNK)
        dtv = _softplus(dt_ref[0, rs, :] + dtbias_ref[...])
        a = dtv * neg_a
        acum = jnp.where(
            fwd_lane,
            jnp.dot(tri_f, a, precision=HIGHEST, preferred_element_type=F32),
            jnp.dot(tri_b, a, precision=HIGHEST, preferred_element_type=F32))
        tot = jnp.where(fwd_lane[0:1, :], acum[CHUNK - 1:CHUNK, :], acum[0:1, :])
        w = jnp.exp(tot - acum) * dtv
        acum_ref[0, rs, :] = acum
        rows_ref[0, c, 0:N_DT, :] = acum.T[0:N_DT, :]
        rows_ref[0, c, N_DT:2 * N_DT, :] = dtv.T[0:N_DT, :]
        rows_ref[0, c, 2 * N_DT:3 * N_DT, :] = w.T[0:N_DT, :]


def _dtprep(dt3, dt_bias, a_log):
    b, l, _ = dt3.shape
    nc = l // CHUNK
    pad = lambda v: jnp.pad(v.reshape(1, N_DT), ((0, 0), (0, LANES - N_DT)))
    vec = pl.BlockSpec((1, LANES), lambda i: (0, 0))
    return pl.pallas_call(
        _dtprep_kernel,
        grid=(b,),
        in_specs=[pl.BlockSpec((1, l, LANES), lambda i: (i, 0, 0)), vec, vec],
        out_specs=[pl.BlockSpec((1, l, LANES), lambda i: (i, 0, 0)),
                   pl.BlockSpec((1, nc, 3 * N_DT, CHUNK), lambda i: (i, 0, 0, 0))],
        out_shape=[jax.ShapeDtypeStruct((b, l, LANES), F32),
                   jax.ShapeDtypeStruct((b, nc, 3 * N_DT, CHUNK), F32)],
        compiler_params=_cparams("arbitrary"),
        name="dtprep",
    )(dt3, pad(dt_bias), pad(a_log))


def _ssd_kernel(xf_ref, bf_ref, cf_ref, af_ref, rf_ref, xb_ref, bb_ref, cb_ref, ab_ref, rb_ref,
                h0f_ref, h0b_ref, yf_ref, yb_ref, stf_ref, stb_ref):
    @pl.when(pl.program_id(1) == 0)
    def _():
        stf_ref[...] = h0f_ref[...]
        stb_ref[...] = h0b_ref[...]

    ri = lax.broadcasted_iota(jnp.int32, (CHUNK, CHUNK), 0)
    ci = lax.broadcasted_iota(jnp.int32, (CHUNK, CHUNK), 1)
    lane_lo = ci < HEAD_DIM
    streams = ((xf_ref, bf_ref, cf_ref, af_ref, rf_ref, yf_ref, stf_ref),
               (xb_ref, bb_ref, cb_ref, ab_ref, rb_ref, yb_ref, stb_ref))
    for d, (x_ref, b_ref, c_ref, a_ref, r_ref, y_ref, st_ref) in enumerate(streams):
        mask = (ci <= ri) if d == 0 else (ci >= ri)
        acum = a_ref[0]
        tot = acum[CHUNK - 1:CHUNK, :] if d == 0 else acum[0:1, :]
        cdec = jnp.exp(tot)
        acum_t = r_ref[0, 0, 0:N_DT, :]
        dtv_t = r_ref[0, 0, N_DT:2 * N_DT, :]
        w_t = r_ref[0, 0, 2 * N_DT:3 * N_DT, :]
        for g in range(GROUPS):
            bg = b_ref[0, :, g * STATE:(g + 1) * STATE]
            cg = c_ref[0, :, g * STATE:(g + 1) * STATE]
            cbm = lax.dot_general(cg, bg, (((1,), (1,)), ((), ())),
                                  preferred_element_type=F32)
            bg_t = bg.astype(F32).T
            for q in range(2):
                cols = slice((2 * g + q) * LANES, (2 * g + q + 1) * LANES)
                xp = x_ref[0, :, cols]
                zero = jnp.zeros_like(xp)
                rhs = jnp.concatenate([jnp.where(lane_lo, xp, zero),
                                       jnp.where(lane_lo, zero, xp)], axis=0)
                lane0 = d * HEADS + g * 4 + q * 2
                m_parts, s_parts, e_parts = [], [], []
                for r2 in range(2):
                    ln = lane0 + r2
                    colb = jnp.broadcast_to(acum[:, ln:ln + 1], (CHUNK, CHUNK))
                    seg = colb - acum_t[ln:ln + 1, :]
                    dec = jnp.exp(jnp.where(mask, seg, -jnp.inf))
                    m_parts.append((cbm * dec * dtv_t[ln:ln + 1, :]).astype(BF16))
                    s_parts.append((bg_t * w_t[ln:ln + 1, :]).astype(BF16))
                    e_parts.append(jnp.exp(colb))
                lhs_y = jnp.concatenate(m_parts, axis=1)
                lhs_s = jnp.concatenate(s_parts, axis=1)
                h_t = st_ref[0, g, :, q * LANES:(q + 1) * LANES]
                y_diag = jnp.dot(lhs_y, rhs, preferred_element_type=F32)
                y_off = (jnp.dot(cg, h_t.astype(BF16), preferred_element_type=F32)
                         * jnp.where(lane_lo, e_parts[0], e_parts[1]))
                y_ref[0, :, cols] = (y_diag + y_off).astype(y_ref.dtype)
                cd = jnp.where(lane_lo[0:1, :],
                               jnp.broadcast_to(cdec[:, lane0:lane0 + 1], (1, LANES)),
                               jnp.broadcast_to(cdec[:, lane0 + 1:lane0 + 2], (1, LANES)))
                st_ref[0, g, :, q * LANES:(q + 1) * LANES] = (
                    h_t * cd + jnp.dot(lhs_s, rhs, preferred_element_type=F32))


def _ssd(xbc, dt3, dt_bias, a_log, h0f, h0b):
    b, l, _ = xbc.shape
    nc = l // CHUNK
    acum, rows = _dtprep(dt3, dt_bias, a_log)
    fwd = lambda k: (lambda i, c: (i, c, k))
    bwd = lambda k: (lambda i, c: (i, nc - 1 - c, k))
    xspec = lambda f: pl.BlockSpec((1, CHUNK, D), f(0))
    bspec = lambda f: pl.BlockSpec((1, CHUNK, GROUPS * STATE), f(2))
    cspec = lambda f: pl.BlockSpec((1, CHUNK, GROUPS * STATE), f(3))
    dspec = lambda f: pl.BlockSpec((1, CHUNK, LANES), f(0))
    rspec = lambda rev: pl.BlockSpec(
        (1, 1, 3 * N_DT, CHUNK),
        (lambda i, c: (i, nc - 1 - c, 0, 0)) if rev else (lambda i, c: (i, c, 0, 0)))
    st = pl.BlockSpec((1, GROUPS, STATE, 4 * HEAD_DIM), lambda i, c: (i, 0, 0, 0))
    return pl.pallas_call(
        _ssd_kernel,
        grid=(b, nc),
        in_specs=[xspec(fwd), bspec(fwd), cspec(fwd), dspec(fwd), rspec(False),
                  xspec(bwd), bspec(bwd), cspec(bwd), dspec(bwd), rspec(True),
                  st, st],
        out_specs=[xspec(fwd), xspec(bwd), st, st],
        out_shape=[jax.ShapeDtypeStruct((b, l, D), BF16),
                   jax.ShapeDtypeStruct((b, l, D), BF16),
                   jax.ShapeDtypeStruct(h0f.shape, F32),
                   jax.ShapeDtypeStruct(h0b.shape, F32)],
        compiler_params=_cparams("arbitrary", "arbitrary"),
        name="ssd",
    )(xbc, xbc, xbc, acum, rows, xbc, xbc, xbc, acum, rows, h0f, h0b)


def _split_bf16(x):
    hi = x.astype(BF16)
    return hi, (x - hi.astype(F32)).astype(BF16)


def _mixer_kernel(sub, yf_ref, yb_ref, xs_ref, z_ref, u_ref, v_ref, ga_ref, gb_ref, h_ref,
                  g1_ref, sc2_ref, sh2_ref, dskip_ref, snw_ref, lnw_ref, lnb_ref, n2w_ref,
                  ws_ref, bs_ref, wpa_ref, wpb_ref, wo_ref, rwh_ref, rwl_ref,
                  hout_ref, xf_ref, afft_ref):
    tm = h_ref.shape[1]
    gw = D // GROUPS
    lane = lax.broadcasted_iota(jnp.int32, (sub, LANES), 1)
    for s in range(tm // sub):
        rs = slice(s * sub, (s + 1) * sub)
        y = (yf_ref[0, rs, :].astype(F32) + yb_ref[0, rs, :].astype(F32)
             + dskip_ref[...] * xs_ref[0, rs, :].astype(F32))
        gated = y * jax.nn.silu(z_ref[0, rs, :].astype(F32))
        ssd = jnp.concatenate(
            [_rms(gated[:, k * gw:(k + 1) * gw], snw_ref[:, k * gw:(k + 1) * gw])
             for k in range(GROUPS)], axis=1)
        u = jax.nn.gelu(u_ref[0, rs, :].astype(F32))
        v = jax.nn.gelu(v_ref[0, rs, :].astype(F32))
        mu = jnp.mean(v, axis=-1, keepdims=True)
        var = jnp.mean(jnp.square(v - mu), axis=-1, keepdims=True)
        vn = ((v - mu) * lax.rsqrt(var + EPS) * lnw_ref[...] + lnb_ref[...]).astype(BF16)
        rows = []
        for ck in range(sub // CHUNK):
            cs = slice(ck * CHUNK, (ck + 1) * CHUNK)
            rows.append(jnp.concatenate(
                [jnp.dot(ws_ref[g], vn[cs, g * LANES:(g + 1) * LANES],
                         preferred_element_type=F32)
                 for g in range(MLP_GROUPS)], axis=1) + bs_ref[...])
        mixed = jnp.concatenate(rows, axis=0) if len(rows) > 1 else rows[0]
        sgu = u * mixed
        pa = jnp.dot(ssd.astype(BF16), wpa_ref[...], preferred_element_type=F32)
        pb = jnp.dot(sgu.astype(BF16), wpb_ref[...], preferred_element_type=F32)
        merged = (jax.nn.sigmoid(ga_ref[0, rs, :].astype(F32)) * pa
                  + jax.nn.sigmoid(gb_ref[0, rs, :].astype(F32)) * pb)
        out = jnp.dot(merged.astype(BF16), wo_ref[...], preferred_element_type=F32)
        hn = h_ref[0, rs, :] + g1_ref[0] * out
        hout_ref[0, rs, :] = hn
        xf = _rms(hn, n2w_ref[...]) * (1.0 + sc2_ref[0]) + sh2_ref[0]
        xf_ref[0, rs, :] = xf.astype(BF16)
        xh, xl = _split_bf16(xf)
        logits = (jnp.dot(xh, rwh_ref[...], preferred_element_type=F32)
                  + jnp.dot(xl, rwh_ref[...], preferred_element_type=F32)
                  + jnp.dot(xh, rwl_ref[...], preferred_element_type=F32))
        logits = jnp.where(lane < N_EXPERTS, logits, -jnp.inf)
        ex = jnp.exp(logits - jnp.max(logits, axis=1, keepdims=True))
        aff = ex / jnp.sum(ex, axis=1, keepdims=True)
        afft_ref[0, :, rs] = aff.T[:N_EXPERTS, :]


def _mixer(yf, yb, xbc, p3, h3, g1, sc2, sh2, consts):
    b, l, _ = h3.shape
    sub = min(256, l)
    tm = min(512, l)
    tok = lambda k: pl.BlockSpec((1, tm, D), lambda i, t: (i, t, k))
    mod = pl.BlockSpec((1, 1, D), (lambda i, t: (i, 0, 0)) if g1.shape[0] > 1
                       else (lambda i, t: (0, 0, 0)))
    const = lambda shape: pl.BlockSpec(shape, lambda i, t: (0,) * len(shape),
                                       pipeline_mode=pl.Buffered(1))
    vec = const((1, D))
    wmat = const((D, D))
    return pl.pallas_call(
        functools.partial(_mixer_kernel, sub),
        grid=(b, l // tm),
        in_specs=[tok(0), tok(0), tok(0), tok(0), tok(1), tok(2), tok(3), tok(4), tok(0),
                  mod, mod, mod, vec, vec, vec, vec, vec,
                  const((MLP_GROUPS, CHUNK, CHUNK)), const((CHUNK, D)),
                  wmat, wmat, wmat, const((D, LANES)), const((D, LANES))],
        out_specs=[tok(0), tok(0),
                   pl.BlockSpec((1, N_EXPERTS, tm), lambda i, t: (i, 0, t))],
        out_shape=[jax.ShapeDtypeStruct((b, l, D), F32),
                   jax.ShapeDtypeStruct((b, l, D), BF16),
                   jax.ShapeDtypeStruct((b, N_EXPERTS, l), F32)],
        compiler_params=_cparams("arbitrary", "arbitrary"),
        name="mixer",
    )(yf, yb, xbc, p3, p3, p3, p3, p3, h3, g1, sc2, sh2, *consts)


def _prefix_excl(m):
    n = m.shape[1]
    tri = (lax.broadcasted_iota(jnp.int32, (LANES, LANES), 0)
           <= lax.broadcasted_iota(jnp.int32, (LANES, LANES), 1)).astype(BF16)
    outs = []
    off = jnp.zeros((m.shape[0], 1), F32)
    for k in range(n // LANES):
        mk = m[:, k * LANES:(k + 1) * LANES]
        inc = jnp.dot(mk.astype(BF16), tri, preferred_element_type=F32)
        outs.append(inc - mk + off)
        off = off + inc[:, LANES - 1:LANES]
    return jnp.concatenate(outs, axis=1) if len(outs) > 1 else outs[0]


def _topk_kernel(cap, aff_ref, pos_ref):
    a = aff_ref[...]
    bits = lax.bitcast_convert_type(a, jnp.int32)
    n_exp = a.shape[0]
    count = lambda pred: jnp.sum(jnp.where(pred, 1.0, 0.0), axis=1, keepdims=True)

    def body(_, carry):
        lo, hi = carry
        mid = lo + ((hi - lo + 1) >> 1)
        ok = count(bits >= mid) >= cap
        return jnp.where(ok, mid, lo), jnp.where(ok, hi, mid - 1)

    lo0 = jnp.zeros((n_exp, 1), jnp.int32)
    hi0 = jnp.full((n_exp, 1), 0x7F800000, jnp.int32)
    thr, _ = lax.fori_loop(0, 31, body, (lo0, hi0))
    gt = bits > thr
    eq = bits == thr
    need = cap - count(gt)
    eq_rank = _prefix_excl(jnp.where(eq, 1.0, 0.0))
    sel = gt | (eq & (eq_rank < need))
    slot = _prefix_excl(jnp.where(sel, 1.0, 0.0))
    pos_ref[...] = jnp.where(sel, slot, -1.0).astype(jnp.int32)


def _topk(aff_t):
    b, n_exp, n = aff_t.shape
    cap = CAPACITY_FACTOR * n // N_EXPERTS
    rows = n_exp * min(4, b)
    spec = pl.BlockSpec((rows, n), lambda i: (i, 0))
    pos = pl.pallas_call(
        functools.partial(_topk_kernel, cap),
        grid=(b * n_exp // rows,),
        in_specs=[spec], out_specs=spec,
        out_shape=jax.ShapeDtypeStruct((b * n_exp, n), jnp.int32),
        compiler_params=_cparams("arbitrary"),
        name="topk",
    )(aff_t.reshape(b * n_exp, n))
    return pos.reshape(b, n_exp, n)


EXPERTS_PER_STEP = 2


def _moe_kernel(cap, xf_ref, aff_ref, pos_ref, wg_ref, wu_ref, wd_ref, o_ref):
    bg, n, _ = xf_ref.shape
    n_local = wg_ref.shape[0]
    step = pl.program_id(1)

    @pl.when(step == 0)
    def _():
        o_ref[...] = jnp.zeros_like(o_ref)

    slot = lax.broadcasted_iota(jnp.int32, (cap, n), 0)
    onehots = [[] for _ in range(bg)]
    ys = [[] for _ in range(bg)]
    for j in range(n_local):
        e = step * n_local + j
        gates, parts = [], []
        for k in range(bg):
            hit = pos_ref[k, pl.ds(e, 1), :] == slot
            onehot = hit.astype(BF16)
            onehots[k].append(onehot)
            parts.append(jnp.dot(onehot, xf_ref[k], preferred_element_type=F32).astype(BF16))
            gates.append(jnp.sum(jnp.where(hit, aff_ref[k, pl.ds(e, 1), :], 0.0),
                                 axis=1, keepdims=True))
        xg = jnp.concatenate(parts, axis=0) if bg > 1 else parts[0]
        hid = (jax.nn.silu(jnp.dot(xg, wg_ref[j], preferred_element_type=F32))
               * jnp.dot(xg, wu_ref[j], preferred_element_type=F32))
        y = jnp.dot(hid.astype(BF16), wd_ref[j], preferred_element_type=F32)
        for k in range(bg):
            ys[k].append((y[k * cap:(k + 1) * cap] * gates[k]).astype(BF16))
    for k in range(bg):
        o_ref[k] += lax.dot_general(jnp.concatenate(onehots[k], axis=0),
                                    jnp.concatenate(ys[k], axis=0),
                                    (((0,), (0,)), ((), ())), preferred_element_type=F32)


def _moe(xf, aff_t, pos, wg, wu, wd, bg):
    b, n, _ = xf.shape
    cap = CAPACITY_FACTOR * n // N_EXPERTS
    tokens = pl.BlockSpec((bg, n, D), lambda i, e: (i, 0, 0))
    routing = pl.BlockSpec((bg, N_EXPERTS, n), lambda i, e: (i, 0, 0))
    wspec = pl.BlockSpec((EXPERTS_PER_STEP, D, D), lambda i, e: (e, 0, 0))
    return pl.pallas_call(
        functools.partial(_moe_kernel, cap),
        grid=(b // bg, N_EXPERTS // EXPERTS_PER_STEP),
        in_specs=[tokens, routing, routing, wspec, wspec, wspec],
        out_specs=tokens,
        out_shape=jax.ShapeDtypeStruct((b, n, D), F32),
        compiler_params=_cparams("arbitrary", "arbitrary"),
        name="moe",
    )(xf, aff_t, pos, wg, wu, wd)


def _final_kernel(h_ref, moe_ref, g_ref, w_ref, o_ref):
    o_ref[0] = _rms(h_ref[0] + g_ref[0] * moe_ref[0], w_ref[...])


def _final(h3, moe3, g2, w):
    b, l, _ = h3.shape
    tm = min(512, l)
    tok = pl.BlockSpec((1, tm, D), lambda i, t: (i, t, 0))
    return pl.pallas_call(
        _final_kernel,
        grid=(b, l // tm),
        in_specs=[tok, tok, pl.BlockSpec((1, 1, D), lambda i, t: (i, 0, 0)),
                  pl.BlockSpec((1, D), lambda i, t: (0, 0))],
        out_specs=tok,
        out_shape=jax.ShapeDtypeStruct(h3.shape, F32),
        compiler_params=_cparams("arbitrary", "arbitrary"),
        name="final",
    )(h3, moe3, g2, w.reshape(1, D))


def kernel(x, c, ctx, c_ctx, ada_w, ada_b, norm1_w, norm2_w, w_in, conv_w, conv_b, dt_bias, a_log,
           d_skip, ssd_norm_w, sgu_ln_w, sgu_ln_b, w_s, b_s, w_pa, w_pb, w_o, router_w,
           w_gate, w_up, w_down, final_norm_w):
    bsz, n_lat, _ = x.shape
    n_ctx = ctx.shape[1]
    depth = ada_w.shape[0]

    mod_rows = ((bsz + 1 + 7) // 8) * 8
    cc = jnp.zeros((mod_rows, D), F32).at[:bsz].set(c).at[bsz].set(c_ctx)
    mods = _ada(cc, ada_w, ada_b)

    zero_state = jnp.zeros((bsz, GROUPS, STATE, 4 * HEAD_DIM), F32)
    h = x.reshape(bsz * n_lat, D)
    hc = ctx.reshape(bsz * n_ctx, D)
    pending_l = None
    pending_c = None
    for layer in range(depth):
        last = layer == depth - 1
        ml = [mods[layer, :bsz, k * D:(k + 1) * D].reshape(bsz, 1, D) for k in range(6)]
        mc = [mods[layer, bsz:bsz + 1, k * D:(k + 1) * D].reshape(1, 1, D) for k in range(6)]
        sh1_l, sc1_l, g1_l, sh2_l, sc2_l, g2_l = ml
        sh1_c, sc1_c, g1_c, sh2_c, sc2_c, g2_c = mc

        w_l = w_in[layer]
        w_main = jnp.concatenate([w_l[:, :N_XBC], w_l[:, N_SCAN:]], axis=1).astype(BF16)
        w_dt = jnp.pad(w_l[:, N_XBC:N_SCAN], ((0, 0), (0, LANES - N_DT))).astype(BF16)
        consts = (jnp.repeat(d_skip[layer], HEAD_DIM).reshape(1, D),
                  ssd_norm_w[layer].reshape(1, D),
                  sgu_ln_w[layer].reshape(1, D), sgu_ln_b[layer].reshape(1, D),
                  norm2_w[layer].reshape(1, D),
                  w_s[layer].astype(BF16),
                  jnp.repeat(b_s[layer].T, LANES, axis=1),
                  w_pa[layer].astype(BF16), w_pb[layer].astype(BF16), w_o[layer].astype(BF16),
                  *_split_bf16(jnp.pad(router_w[layer], ((0, 0), (0, LANES - N_EXPERTS)))))
        wg = _to_bf16(w_gate, layer)
        wu = _to_bf16(w_up, layer)
        wd = _to_bf16(w_down, layer)
        proj = functools.partial(_in_proj, norm_w=norm1_w[layer], w_dt=w_dt,
                                 conv_w=conv_w[layer], conv_b=conv_b[layer])
        scan = functools.partial(_ssd, dt_bias=dt_bias[layer], a_log=a_log[layer])

        out_c = list(proj(hc, sc1_c, sh1_c, w_main=w_main[:, :N_XBC] if last else w_main,
                          seq_len=n_ctx, tokens_per_mod=bsz * n_ctx, res=pending_c))
        if pending_c is not None:
            hc = out_c.pop(2)
        xbc_c = out_c[0].reshape(bsz, n_ctx, N_XBC)
        yf_c, yb_c, st_f, st_b = scan(xbc_c, out_c[1].reshape(bsz, n_ctx, LANES),
                                      h0f=zero_state, h0b=zero_state)

        out_l = list(proj(h, sc1_l, sh1_l, w_main=w_main, seq_len=n_lat, tokens_per_mod=n_lat,
                          res=pending_l))
        if pending_l is not None:
            h = out_l.pop(2)
        xbc_l = out_l[0].reshape(bsz, n_lat, N_XBC)
        yf_l, yb_l, _, _ = scan(xbc_l, out_l[1].reshape(bsz, n_lat, LANES), h0f=st_f, h0b=st_b)
        h3, xf_l, aff_l = _mixer(yf_l, yb_l, xbc_l, out_l[2].reshape(bsz, n_lat, N_REST),
                                 h.reshape(bsz, n_lat, D), g1_l, sc2_l, sh2_l, consts)
        moe_l = _moe(xf_l, aff_l, _topk(aff_l), wg, wu, wd, bg=1)
        h = h3.reshape(bsz * n_lat, D)
        pending_l = (moe_l.reshape(bsz * n_lat, D), g2_l)

        if not last:
            hc3, xf_c, aff_c = _mixer(yf_c, yb_c, xbc_c, out_c[2].reshape(bsz, n_ctx, N_REST),
                                      hc.reshape(bsz, n_ctx, D), g1_c, sc2_c, sh2_c, consts)
            moe_c = _moe(xf_c, aff_c, _topk(aff_c), wg, wu, wd, bg=min(8, bsz))
            hc = hc3.reshape(bsz * n_ctx, D)
            pending_c = (moe_c.reshape(bsz * n_ctx, D), g2_c)

    moe_l, g2_l = pending_l
    return _final(h.reshape(bsz, n_lat, D), moe_l.reshape(bsz, n_lat, D), g2_l, final_norm_w)
```

```python
import functools

import jax
import jax.numpy as jnp
from jax import lax
from jax.experimental import pallas as pl
from jax.experimental.pallas import tpu as pltpu

F32 = jnp.float32
BF16 = jnp.bfloat16
HIGHEST = lax.Precision.HIGHEST

D = 1024
CHUNK = 128
EPS = 1e-6
HEADS = 16
HEAD_DIM = 64
GROUPS = 4
STATE = 128
N_XBC = 2048
N_DT = 2 * HEADS
N_SCAN = N_XBC + N_DT
N_REST = 5 * D
MLP_GROUPS = 8
N_EXPERTS = 16
CAPACITY_FACTOR = 2
LANES = 128
VMEM_LIMIT = 56 * 1024 * 1024


def _cparams(*sem):
    return pltpu.CompilerParams(dimension_semantics=sem, vmem_limit_bytes=VMEM_LIMIT)


def _softplus(x):
    return jnp.maximum(x, 0.0) + jnp.log1p(jnp.exp(-jnp.abs(x)))


def _rms(x, w):
    return x * lax.rsqrt(jnp.mean(x * x, axis=-1, keepdims=True) + EPS) * w


def _cast_kernel(w_ref, o_ref):
    o_ref[...] = w_ref[0].astype(o_ref.dtype)


def _to_bf16(w, layer):
    g = w.shape[1]
    return pl.pallas_call(
        _cast_kernel,
        grid=(g, 2),
        in_specs=[pl.BlockSpec((1, 1, D // 2, D), lambda i, j: (layer, i, j, 0))],
        out_specs=pl.BlockSpec((1, D // 2, D), lambda i, j: (i, j, 0)),
        out_shape=jax.ShapeDtypeStruct(w.shape[1:], BF16),
        compiler_params=_cparams("arbitrary", "arbitrary"),
        name="to_bf16",
    )(w)


def _ada_kernel(c_ref, w_ref, b_ref, o_ref):
    a = jax.nn.silu(c_ref[...]).astype(BF16)
    o_ref[0] = jnp.dot(a, w_ref[0].astype(BF16), preferred_element_type=F32) + b_ref[0]


def _ada(cc, ada_w, ada_b):
    n_layers = ada_w.shape[0]
    rows = cc.shape[0]
    return pl.pallas_call(
        _ada_kernel,
        grid=(n_layers, 6),
        in_specs=[pl.BlockSpec((rows, D), lambda l, j: (0, 0)),
                  pl.BlockSpec((1, D, D), lambda l, j: (l, 0, j)),
                  pl.BlockSpec((1, 1, D), lambda l, j: (l, 0, j))],
        out_specs=pl.BlockSpec((1, rows, D), lambda l, j: (l, 0, j)),
        out_shape=jax.ShapeDtypeStruct((n_layers, rows, 6 * D), F32),
        compiler_params=_cparams("arbitrary", "arbitrary"),
        name="ada",
    )(cc, ada_w, ada_b.reshape(n_layers, 1, 6 * D))


HALO = 8


def _in_proj_kernel(seq_len, with_res, *refs):
    if with_res:
        (hp_ref, h_ref, hn_ref, mp_ref, m_ref, mn_ref, g_ref, sc_ref, sh_ref, nw_ref, w_ref,
         wdt_ref, cw_ref, cb_ref, xbc_ref, dt_ref, hout_ref, *rest) = refs
        h = h_ref[...] + g_ref[0] * m_ref[...]
        hout_ref[...] = h
        he = (jnp.concatenate([hp_ref[...], hn_ref[...]], axis=0)
              + g_ref[0] * jnp.concatenate([mp_ref[...], mn_ref[...]], axis=0))
    else:
        (hp_ref, h_ref, hn_ref, sc_ref, sh_ref, nw_ref, w_ref,
         wdt_ref, cw_ref, cb_ref, xbc_ref, dt_ref, *rest) = refs
        h = h_ref[...]
        he = jnp.concatenate([hp_ref[...], hn_ref[...]], axis=0)
    tm = h.shape[0]
    win_ref = rest[-1]
    norm = lambda v: (_rms(v, nw_ref[...]) * (1.0 + sc_ref[0]) + sh_ref[0]).astype(BF16)
    xm = norm(h)
    xe = norm(he)
    dt_ref[...] = jnp.dot(xm, wdt_ref[...], preferred_element_type=F32)

    row = lax.broadcasted_iota(jnp.int32, (tm, D), 0)
    if seq_len >= tm:
        pos = (pl.program_id(0) % (seq_len // tm)) * tm + row
        first = pos == 0
        last = pos == seq_len - 1
    else:
        first = functools.reduce(jnp.logical_or,
                                 [row == k * seq_len for k in range(tm // seq_len)])
        last = functools.reduce(jnp.logical_or,
                                [row == (k + 1) * seq_len - 1 for k in range(tm // seq_len)])
    for j in range(N_XBC // D):
        cols = slice(j * D, (j + 1) * D)
        pm = jnp.dot(xm, w_ref[:, cols], preferred_element_type=F32)
        pe = jnp.dot(xe, w_ref[:, cols], preferred_element_type=F32)
        win = win_ref.at[j]
        win[0:HALO, :] = pe[:HALO]
        win[HALO:HALO + tm, :] = pm
        win[HALO + tm:, :] = pe[HALO:]
        prev = win[HALO - 1:HALO - 1 + tm, :]
        nxt = win[HALO + 1:HALO + 1 + tm, :]
        y = (jnp.where(first, 0.0, prev) * cw_ref[0:1, cols] + pm * cw_ref[1:2, cols]
             + jnp.where(last, 0.0, nxt) * cw_ref[2:3, cols] + cb_ref[:, cols])
        xbc_ref[:, cols] = jax.nn.silu(y).astype(BF16)
    if len(rest) > 1:
        p_ref = rest[0]
        for j in range(p_ref.shape[1] // D):
            cols = slice(N_XBC + j * D, N_XBC + (j + 1) * D)
            p_ref[:, j * D:(j + 1) * D] = jnp.dot(
                xm, w_ref[:, cols], preferred_element_type=F32).astype(BF16)


def _in_proj(h2, sc, sh, norm_w, w_main, w_dt, conv_w, conv_b, seq_len, tokens_per_mod, res=None):
    m = h2.shape[0]
    n_rest = w_main.shape[1] - N_XBC
    tm = min(512, tokens_per_mod)
    tiles_per_mod = tokens_per_mod // tm
    hb = tm // HALO
    with_res = res is not None
    row = lambda width: pl.BlockSpec((tm, width), lambda i: (i, 0))
    halo_prev = pl.BlockSpec((HALO, D), lambda i: (jnp.maximum(i * hb - 1, 0), 0))
    halo_next = pl.BlockSpec((HALO, D), lambda i: (jnp.minimum((i + 1) * hb, m // HALO - 1), 0))
    mod = pl.BlockSpec((1, 1, D), lambda i: (i // tiles_per_mod, 0, 0))
    resident = lambda shape: pl.BlockSpec(shape, lambda i: (0, 0), pipeline_mode=pl.Buffered(1))
    in_specs = [halo_prev, row(D), halo_next]
    args = [h2, h2, h2]
    if with_res:
        in_specs += [halo_prev, row(D), halo_next, mod]
        args += [res[0], res[0], res[0], res[1]]
    in_specs += [mod, mod, resident((1, D)), resident(w_main.shape), resident((D, LANES)),
                 resident((3, N_XBC)), resident((1, N_XBC))]
    args += [sc, sh, norm_w.reshape(1, D), w_main, w_dt, conv_w, conv_b.reshape(1, N_XBC)]
    out_specs = [row(N_XBC), row(LANES)]
    out_shape = [jax.ShapeDtypeStruct((m, N_XBC), BF16), jax.ShapeDtypeStruct((m, LANES), F32)]
    if with_res:
        out_specs.append(row(D))
        out_shape.append(jax.ShapeDtypeStruct((m, D), F32))
    if n_rest:
        out_specs.append(row(n_rest))
        out_shape.append(jax.ShapeDtypeStruct((m, n_rest), BF16))
    return pl.pallas_call(
        functools.partial(_in_proj_kernel, seq_len, with_res),
        grid=(m // tm,),
        in_specs=in_specs, out_specs=out_specs, out_shape=out_shape,
        scratch_shapes=[pltpu.VMEM((N_XBC // D, tm + 2 * HALO, D), F32)],
        compiler_params=_cparams("arbitrary"),
        name="in_proj_res" if with_res else "in_proj",
    )(*args)


def _dtprep_kernel(dt_ref, dtbias_ref, alog_ref, acum_ref, rows_ref):
    nc = rows_ref.shape[1]
    ri = lax.broadcasted_iota(jnp.int32, (CHUNK, CHUNK), 0)
    ci = lax.broadcasted_iota(jnp.int32, (CHUNK, CHUNK), 1)
    tri_f = (ci <= ri).astype(F32)
    tri_b = (ci >= ri).astype(F32)
    fwd_lane = lax.broadcasted_iota(jnp.int32, (CHUNK, LANES), 1) < HEADS
    neg_a = -jnp.exp(alog_ref[...])
    for c in range(nc):
        rs = slice(c * CHUNK, (c + 1) * CHUNK)
        dtv = _softplus(dt_ref[0, rs, :] + dtbias_ref[...])
        a = dtv * neg_a
        acum = jnp.where(
            fwd_lane,
            jnp.dot(tri_f, a, precision=HIGHEST, preferred_element_type=F32),
            jnp.dot(tri_b, a, precision=HIGHEST, preferred_element_type=F32))
        tot = jnp.where(fwd_lane[0:1, :], acum[CHUNK - 1:CHUNK, :], acum[0:1, :])
        w = jnp.exp(tot - acum) * dtv
        acum_ref[0, rs, :] = acum
        rows_ref[0, c, 0:N_DT, :] = acum.T[0:N_DT, :]
        rows_ref[0, c, N_DT:2 * N_DT, :] = dtv.T[0:N_DT, :]
        rows_ref[0, c, 2 * N_DT:3 * N_DT, :] = w.T[0:N_DT, :]


def _dtprep(dt3, dt_bias, a_log):
    b, l, _ = dt3.shape
    nc = l // CHUNK
    pad = lambda v: jnp.pad(v.reshape(1, N_DT), ((0, 0), (0, LANES - N_DT)))
    vec = pl.BlockSpec((1, LANES), lambda i: (0, 0))
    return pl.pallas_call(
        _dtprep_kernel,
        grid=(b,),
        in_specs=[pl.BlockSpec((1, l, LANES), lambda i: (i, 0, 0)), vec, vec],
        out_specs=[pl.BlockSpec((1, l, LANES), lambda i: (i, 0, 0)),
                   pl.BlockSpec((1, nc, 3 * N_DT, CHUNK), lambda i: (i, 0, 0, 0))],
        out_shape=[jax.ShapeDtypeStruct((b, l, LANES), F32),
                   jax.ShapeDtypeStruct((b, nc, 3 * N_DT, CHUNK), F32)],
        compiler_params=_cparams("arbitrary"),
        name="dtprep",
    )(dt3, pad(dt_bias), pad(a_log))


def _ssd_kernel(xf_ref, bf_ref, cf_ref, af_ref, rf_ref, xb_ref, bb_ref, cb_ref, ab_ref, rb_ref,
                h0f_ref, h0b_ref, yf_ref, yb_ref, stf_ref, stb_ref):
    @pl.when(pl.program_id(1) == 0)
    def _():
        stf_ref[...] = h0f_ref[...]
        stb_ref[...] = h0b_ref[...]

    ri = lax.broadcasted_iota(jnp.int32, (CHUNK, CHUNK), 0)
    ci = lax.broadcasted_iota(jnp.int32, (CHUNK, CHUNK), 1)
    lane_lo = ci < HEAD_DIM
    streams = ((xf_ref, bf_ref, cf_ref, af_ref, rf_ref, yf_ref, stf_ref),
               (xb_ref, bb_ref, cb_ref, ab_ref, rb_ref, yb_ref, stb_ref))
    for d, (x_ref, b_ref, c_ref, a_ref, r_ref, y_ref, st_ref) in enumerate(streams):
        mask = (ci <= ri) if d == 0 else (ci >= ri)
        acum = a_ref[0]
        tot = acum[CHUNK - 1:CHUNK, :] if d == 0 else acum[0:1, :]
        cdec = jnp.exp(tot)
        acum_t = r_ref[0, 0, 0:N_DT, :]
        dtv_t = r_ref[0, 0, N_DT:2 * N_DT, :]
        w_t = r_ref[0, 0, 2 * N_DT:3 * N_DT, :]
        for g in range(GROUPS):
            bg = b_ref[0, :, g * STATE:(g + 1) * STATE]
            cg = c_ref[0, :, g * STATE:(g + 1) * STATE]
            cbm = lax.dot_general(cg, bg, (((1,), (1,)), ((), ())),
                                  preferred_element_type=F32)
            bg_t = bg.astype(F32).T
            for q in range(2):
                cols = slice((2 * g + q) * LANES, (2 * g + q + 1) * LANES)
                xp = x_ref[0, :, cols]
                zero = jnp.zeros_like(xp)
                rhs = jnp.concatenate([jnp.where(lane_lo, xp, zero),
                                       jnp.where(lane_lo, zero, xp)], axis=0)
                lane0 = d * HEADS + g * 4 + q * 2
                m_parts, s_parts, e_parts = [], [], []
                for r2 in range(2):
                    ln = lane0 + r2
                    colb = jnp.broadcast_to(acum[:, ln:ln + 1], (CHUNK, CHUNK))
                    seg = colb - acum_t[ln:ln + 1, :]
                    dec = jnp.exp(jnp.where(mask, seg, -jnp.inf))
                    m_parts.append((cbm * dec * dtv_t[ln:ln + 1, :]).astype(BF16))
                    s_parts.append((bg_t * w_t[ln:ln + 1, :]).astype(BF16))
                    e_parts.append(jnp.exp(colb))
                lhs_y = jnp.concatenate(m_parts, axis=1)
                lhs_s = jnp.concatenate(s_parts, axis=1)
                h_t = st_ref[0, g, :, q * LANES:(q + 1) * LANES]
                y_diag = jnp.dot(lhs_y, rhs, preferred_element_type=F32)
                y_off = (jnp.dot(cg, h_t.astype(BF16), preferred_element_type=F32)
                         * jnp.where(lane_lo, e_parts[0], e_parts[1]))
                y_ref[0, :, cols] = (y_diag + y_off).astype(y_ref.dtype)
                cd = jnp.where(lane_lo[0:1, :],
                               jnp.broadcast_to(cdec[:, lane0:lane0 + 1], (1, LANES)),
                               jnp.broadcast_to(cdec[:, lane0 + 1:lane0 + 2], (1, LANES)))
                st_ref[0, g, :, q * LANES:(q + 1) * LANES] = (
                    h_t * cd + jnp.dot(lhs_s, rhs, preferred_element_type=F32))


def _ssd(xbc, dt3, dt_bias, a_log, h0f, h0b):
    b, l, _ = xbc.shape
    nc = l // CHUNK
    acum, rows = _dtprep(dt3, dt_bias, a_log)
    fwd = lambda k: (lambda i, c: (i, c, k))
    bwd = lambda k: (lambda i, c: (i, nc - 1 - c, k))
    xspec = lambda f: pl.BlockSpec((1, CHUNK, D), f(0))
    bspec = lambda f: pl.BlockSpec((1, CHUNK, GROUPS * STATE), f(2))
    cspec = lambda f: pl.BlockSpec((1, CHUNK, GROUPS * STATE), f(3))
    dspec = lambda f: pl.BlockSpec((1, CHUNK, LANES), f(0))
    rspec = lambda rev: pl.BlockSpec(
        (1, 1, 3 * N_DT, CHUNK),
        (lambda i, c: (i, nc - 1 - c, 0, 0)) if rev else (lambda i, c: (i, c, 0, 0)))
    st = pl.BlockSpec((1, GROUPS, STATE, 4 * HEAD_DIM), lambda i, c: (i, 0, 0, 0))
    return pl.pallas_call(
        _ssd_kernel,
        grid=(b, nc),
        in_specs=[xspec(fwd), bspec(fwd), cspec(fwd), dspec(fwd), rspec(False),
                  xspec(bwd), bspec(bwd), cspec(bwd), dspec(bwd), rspec(True),
                  st, st],
        out_specs=[xspec(fwd), xspec(bwd), st, st],
        out_shape=[jax.ShapeDtypeStruct((b, l, D), BF16),
                   jax.ShapeDtypeStruct((b, l, D), BF16),
                   jax.ShapeDtypeStruct(h0f.shape, F32),
                   jax.ShapeDtypeStruct(h0b.shape, F32)],
        compiler_params=_cparams("arbitrary", "arbitrary"),
        name="ssd",
    )(xbc, xbc, xbc, acum, rows, xbc, xbc, xbc, acum, rows, h0f, h0b)


def _split_bf16(x):
    hi = x.astype(BF16)
    return hi, (x - hi.astype(F32)).astype(BF16)


def _mixer_kernel(sub, yf_ref, yb_ref, xs_ref, z_ref, u_ref, v_ref, ga_ref, gb_ref, h_ref,
                  g1_ref, sc2_ref, sh2_ref, dskip_ref, snw_ref, lnw_ref, lnb_ref, n2w_ref,
                  ws_ref, bs_ref, wpa_ref, wpb_ref, wo_ref, rwh_ref, rwl_ref,
                  hout_ref, xf_ref, afft_ref):
    tm = h_ref.shape[1]
    gw = D // GROUPS
    lane = lax.broadcasted_iota(jnp.int32, (sub, LANES), 1)
    for s in range(tm // sub):
        rs = slice(s * sub, (s + 1) * sub)
        y = (yf_ref[0, rs, :].astype(F32) + yb_ref[0, rs, :].astype(F32)
             + dskip_ref[...] * xs_ref[0, rs, :].astype(F32))
        gated = y * jax.nn.silu(z_ref[0, rs, :].astype(F32))
        ssd = jnp.concatenate(
            [_rms(gated[:, k * gw:(k + 1) * gw], snw_ref[:, k * gw:(k + 1) * gw])
             for k in range(GROUPS)], axis=1)
        u = jax.nn.gelu(u_ref[0, rs, :].astype(F32))
        v = jax.nn.gelu(v_ref[0, rs, :].astype(F32))
        mu = jnp.mean(v, axis=-1, keepdims=True)
        var = jnp.mean(jnp.square(v - mu), axis=-1, keepdims=True)
        vn = ((v - mu) * lax.rsqrt(var + EPS) * lnw_ref[...] + lnb_ref[...]).astype(BF16)
        rows = []
        for ck in range(sub // CHUNK):
            cs = slice(ck * CHUNK, (ck + 1) * CHUNK)
            rows.append(jnp.concatenate(
                [jnp.dot(ws_ref[g], vn[cs, g * LANES:(g + 1) * LANES],
                         preferred_element_type=F32)
                 for g in range(MLP_GROUPS)], axis=1) + bs_ref[...])
        mixed = jnp.concatenate(rows, axis=0) if len(rows) > 1 else rows[0]
        sgu = u * mixed
        pa = jnp.dot(ssd.astype(BF16), wpa_ref[...], preferred_element_type=F32)
        pb = jnp.dot(sgu.astype(BF16), wpb_ref[...], preferred_element_type=F32)
        merged = (jax.nn.sigmoid(ga_ref[0, rs, :].astype(F32)) * pa
                  + jax.nn.sigmoid(gb_ref[0, rs, :].astype(F32)) * pb)
        out = jnp.dot(merged.astype(BF16), wo_ref[...], preferred_element_type=F32)
        hn = h_ref[0, rs, :] + g1_ref[0] * out
        hout_ref[0, rs, :] = hn
        xf = _rms(hn, n2w_ref[...]) * (1.0 + sc2_ref[0]) + sh2_ref[0]
        xf_ref[0, rs, :] = xf.astype(BF16)
        xh, xl = _split_bf16(xf)
        logits = (jnp.dot(xh, rwh_ref[...], preferred_element_type=F32)
                  + jnp.dot(xl, rwh_ref[...], preferred_element_type=F32)
                  + jnp.dot(xh, rwl_ref[...], preferred_element_type=F32))
        logits = jnp.where(lane < N_EXPERTS, logits, -jnp.inf)
        ex = jnp.exp(logits - jnp.max(logits, axis=1, keepdims=True))
        aff = ex / jnp.sum(ex, axis=1, keepdims=True)
        afft_ref[0, :, rs] = aff.T[:N_EXPERTS, :]


def _mixer(yf, yb, xbc, p3, h3, g1, sc2, sh2, consts):
    b, l, _ = h3.shape
    sub = min(256, l)
    tm = min(512, l)
    tok = lambda k: pl.BlockSpec((1, tm, D), lambda i, t: (i, t, k))
    mod = pl.BlockSpec((1, 1, D), (lambda i, t: (i, 0, 0)) if g1.shape[0] > 1
                       else (lambda i, t: (0, 0, 0)))
    const = lambda shape: pl.BlockSpec(shape, lambda i, t: (0,) * len(shape),
                                       pipeline_mode=pl.Buffered(1))
    vec = const((1, D))
    wmat = const((D, D))
    return pl.pallas_call(
        functools.partial(_mixer_kernel, sub),
        grid=(b, l // tm),
        in_specs=[tok(0), tok(0), tok(0), tok(0), tok(1), tok(2), tok(3), tok(4), tok(0),
                  mod, mod, mod, vec, vec, vec, vec, vec,
                  const((MLP_GROUPS, CHUNK, CHUNK)), const((CHUNK, D)),
                  wmat, wmat, wmat, const((D, LANES)), const((D, LANES))],
        out_specs=[tok(0), tok(0),
                   pl.BlockSpec((1, N_EXPERTS, tm), lambda i, t: (i, 0, t))],
        out_shape=[jax.ShapeDtypeStruct((b, l, D), F32),
                   jax.ShapeDtypeStruct((b, l, D), BF16),
                   jax.ShapeDtypeStruct((b, N_EXPERTS, l), F32)],
        compiler_params=_cparams("arbitrary", "arbitrary"),
        name="mixer",
    )(yf, yb, xbc, p3, p3, p3, p3, p3, h3, g1, sc2, sh2, *consts)


def _prefix_excl(m):
    n = m.shape[1]
    tri = (lax.broadcasted_iota(jnp.int32, (LANES, LANES), 0)
           <= lax.broadcasted_iota(jnp.int32, (LANES, LANES), 1)).astype(BF16)
    outs = []
    off = jnp.zeros((m.shape[0], 1), F32)
    for k in range(n // LANES):
        mk = m[:, k * LANES:(k + 1) * LANES]
        inc = jnp.dot(mk.astype(BF16), tri, preferred_element_type=F32)
        outs.append(inc - mk + off)
        off = off + inc[:, LANES - 1:LANES]
    return jnp.concatenate(outs, axis=1) if len(outs) > 1 else outs[0]


def _topk_kernel(cap, aff_ref, pos_ref):
    a = aff_ref[...]
    bits = lax.bitcast_convert_type(a, jnp.int32)
    n_exp = a.shape[0]
    count = lambda pred: jnp.sum(jnp.where(pred, 1.0, 0.0), axis=1, keepdims=True)

    def body(_, carry):
        lo, hi = carry
        mid = lo + ((hi - lo + 1) >> 1)
        ok = count(bits >= mid) >= cap
        return jnp.where(ok, mid, lo), jnp.where(ok, hi, mid - 1)

    lo0 = jnp.zeros((n_exp, 1), jnp.int32)
    hi0 = jnp.full((n_exp, 1), 0x7F800000, jnp.int32)
    thr, _ = lax.fori_loop(0, 31, body, (lo0, hi0))
    gt = bits > thr
    eq = bits == thr
    need = cap - count(gt)
    eq_rank = _prefix_excl(jnp.where(eq, 1.0, 0.0))
    sel = gt | (eq & (eq_rank < need))
    slot = _prefix_excl(jnp.where(sel, 1.0, 0.0))
    pos_ref[...] = jnp.where(sel, slot, -1.0).astype(jnp.int32)


def _topk(aff_t):
    b, n_exp, n = aff_t.shape
    cap = CAPACITY_FACTOR * n // N_EXPERTS
    rows = n_exp * min(4, b)
    spec = pl.BlockSpec((rows, n), lambda i: (i, 0))
    pos = pl.pallas_call(
        functools.partial(_topk_kernel, cap),
        grid=(b * n_exp // rows,),
        in_specs=[spec], out_specs=spec,
        out_shape=jax.ShapeDtypeStruct((b * n_exp, n), jnp.int32),
        compiler_params=_cparams("arbitrary"),
        name="topk",
    )(aff_t.reshape(b * n_exp, n))
    return pos.reshape(b, n_exp, n)


EXPERTS_PER_STEP = 2


def _moe_kernel(cap, xf_ref, aff_ref, pos_ref, wg_ref, wu_ref, wd_ref, o_ref):
    bg, n, _ = xf_ref.shape
    n_local = wg_ref.shape[0]
    step = pl.program_id(1)

    @pl.when(step == 0)
    def _():
        o_ref[...] = jnp.zeros_like(o_ref)

    slot = lax.broadcasted_iota(jnp.int32, (cap, n), 0)
    onehots = [[] for _ in range(bg)]
    ys = [[] for _ in range(bg)]
    for j in range(n_local):
        e = step * n_local + j
        gates, parts = [], []
        for k in range(bg):
            hit = pos_ref[k, pl.ds(e, 1), :] == slot
            onehot = hit.astype(BF16)
            onehots[k].append(onehot)
            parts.append(jnp.dot(onehot, xf_ref[k], preferred_element_type=F32).astype(BF16))
            gates.append(jnp.sum(jnp.where(hit, aff_ref[k, pl.ds(e, 1), :], 0.0),
                                 axis=1, keepdims=True))
        xg = jnp.concatenate(parts, axis=0) if bg > 1 else parts[0]
        hid = (jax.nn.silu(jnp.dot(xg, wg_ref[j], preferred_element_type=F32))
               * jnp.dot(xg, wu_ref[j], preferred_element_type=F32))
        y = jnp.dot(hid.astype(BF16), wd_ref[j], preferred_element_type=F32)
        for k in range(bg):
            ys[k].append((y[k * cap:(k + 1) * cap] * gates[k]).astype(BF16))
    for k in range(bg):
        o_ref[k] += lax.dot_general(jnp.concatenate(onehots[k], axis=0),
                                    jnp.concatenate(ys[k], axis=0),
                                    (((0,), (0,)), ((), ())), preferred_element_type=F32)


def _moe(xf, aff_t, pos, wg, wu, wd, bg):
    b, n, _ = xf.shape
    cap = CAPACITY_FACTOR * n // N_EXPERTS
    tokens = pl.BlockSpec((bg, n, D), lambda i, e: (i, 0, 0))
    routing = pl.BlockSpec((bg, N_EXPERTS, n), lambda i, e: (i, 0, 0))
    wspec = pl.BlockSpec((EXPERTS_PER_STEP, D, D), lambda i, e: (e, 0, 0))
    return pl.pallas_call(
        functools.partial(_moe_kernel, cap),
        grid=(b // bg, N_EXPERTS // EXPERTS_PER_STEP),
        in_specs=[tokens, routing, routing, wspec, wspec, wspec],
        out_specs=tokens,
        out_shape=jax.ShapeDtypeStruct((b, n, D), F32),
        compiler_params=_cparams("arbitrary", "arbitrary"),
        name="moe",
    )(xf, aff_t, pos, wg, wu, wd)


def _final_kernel(h_ref, moe_ref, g_ref, w_ref, o_ref):
    o_ref[0] = _rms(h_ref[0] + g_ref[0] * moe_ref[0], w_ref[...])


def _final(h3, moe3, g2, w):
    b, l, _ = h3.shape
    tm = min(512, l)
    tok = pl.BlockSpec((1, tm, D), lambda i, t: (i, t, 0))
    return pl.pallas_call(
        _final_kernel,
        grid=(b, l // tm),
        in_specs=[tok, tok, pl.BlockSpec((1, 1, D), lambda i, t: (i, 0, 0)),
                  pl.BlockSpec((1, D), lambda i, t: (0, 0))],
        out_specs=tok,
        out_shape=jax.ShapeDtypeStruct(h3.shape, F32),
        compiler_params=_cparams("arbitrary", "arbitrary"),
        name="final",
    )(h3, moe3, g2, w.reshape(1, D))


def kernel(x, c, ctx, c_ctx, ada_w, ada_b, norm1_w, norm2_w, w_in, conv_w, conv_b, dt_bias, a_log,
           d_skip, ssd_norm_w, sgu_ln_w, sgu_ln_b, w_s, b_s, w_pa, w_pb, w_o, router_w,
           w_gate, w_up, w_down, final_norm_w):
    bsz, n_lat, _ = x.shape
    n_ctx = ctx.shape[1]
    depth = ada_w.shape[0]

    mod_rows = ((bsz + 1 + 7) // 8) * 8
    cc = jnp.zeros((mod_rows, D), F32).at[:bsz].set(c).at[bsz].set(c_ctx)
    mods = _ada(cc, ada_w, ada_b)

    zero_state = jnp.zeros((bsz, GROUPS, STATE, 4 * HEAD_DIM), F32)
    h = x.reshape(bsz * n_lat, D)
    hc = ctx.reshape(bsz * n_ctx, D)
    pending_l = None
    pending_c = None
    for layer in range(depth):
        last = layer == depth - 1
        ml = [mods[layer, :bsz, k * D:(k + 1) * D].reshape(bsz, 1, D) for k in range(6)]
        mc = [mods[layer, bsz:bsz + 1, k * D:(k + 1) * D].reshape(1, 1, D) for k in range(6)]
        sh1_l, sc1_l, g1_l, sh2_l, sc2_l, g2_l = ml
        sh1_c, sc1_c, g1_c, sh2_c, sc2_c, g2_c = mc

        w_l = w_in[layer]
        w_main = jnp.concatenate([w_l[:, :N_XBC], w_l[:, N_SCAN:]], axis=1).astype(BF16)
        w_dt = jnp.pad(w_l[:, N_XBC:N_SCAN], ((0, 0), (0, LANES - N_DT))).astype(BF16)
        consts = (jnp.repeat(d_skip[layer], HEAD_DIM).reshape(1, D),
                  ssd_norm_w[layer].reshape(1, D),
                  sgu_ln_w[layer].reshape(1, D), sgu_ln_b[layer].reshape(1, D),
                  norm2_w[layer].reshape(1, D),
                  w_s[layer].astype(BF16),
                  jnp.repeat(b_s[layer].T, LANES, axis=1),
                  w_pa[layer].astype(BF16), w_pb[layer].astype(BF16), w_o[layer].astype(BF16),
                  *_split_bf16(jnp.pad(router_w[layer], ((0, 0), (0, LANES - N_EXPERTS)))))
        wg = _to_bf16(w_gate, layer)
        wu = _to_bf16(w_up, layer)
        wd = _to_bf16(w_down, layer)
        proj = functools.partial(_in_proj, norm_w=norm1_w[layer], w_dt=w_dt,
                                 conv_w=conv_w[layer], conv_b=conv_b[layer])
        scan = functools.partial(_ssd, dt_bias=dt_bias[layer], a_log=a_log[layer])

        out_c = list(proj(hc, sc1_c, sh1_c, w_main=w_main[:, :N_XBC] if last else w_main,
                          seq_len=n_ctx, tokens_per_mod=bsz * n_ctx, res=pending_c))
        if pending_c is not None:
            hc = out_c.pop(2)
        xbc_c = out_c[0].reshape(bsz, n_ctx, N_XBC)
        yf_c, yb_c, st_f, st_b = scan(xbc_c, out_c[1].reshape(bsz, n_ctx, LANES),
                                      h0f=zero_state, h0b=zero_state)

        out_l = list(proj(h, sc1_l, sh1_l, w_main=w_main, seq_len=n_lat, tokens_per_mod=n_lat,
                          res=pending_l))
        if pending_l is not None:
            h = out_l.pop(2)
        xbc_l = out_l[0].reshape(bsz, n_lat, N_XBC)
        yf_l, yb_l, _, _ = scan(xbc_l, out_l[1].reshape(bsz, n_lat, LANES), h0f=st_f, h0b=st_b)
        h3, xf_l, aff_l = _mixer(yf_l, yb_l, xbc_l, out_l[2].reshape(bsz, n_lat, N_REST),
                                 h.reshape(bsz, n_lat, D), g1_l, sc2_l, sh2_l, consts)
        moe_l = _moe(xf_l, aff_l, _topk(aff_l), wg, wu, wd, bg=1)
        h = h3.reshape(bsz * n_lat, D)
        pending_l = (moe_l.reshape(bsz * n_lat, D), g2_l)

        if not last:
            hc3, xf_c, aff_c = _mixer(yf_c, yb_c, xbc_c, out_c[2].reshape(bsz, n_ctx, N_REST),
                                      hc.reshape(bsz, n_ctx, D), g1_c, sc2_c, sh2_c, consts)
            moe_c = _moe(xf_c, aff_c, _topk(aff_c), wg, wu, wd, bg=min(8, bsz))
            hc = hc3.reshape(bsz * n_ctx, D)
            pending_c = (moe_c.reshape(bsz * n_ctx, D), g2_c)

    moe_l, g2_l = pending_l
    return _final(h.reshape(bsz, n_lat, D), moe_l.reshape(bsz, n_lat, D), g2_l, final_norm_w)
```

```python
import functools

import jax
import jax.numpy as jnp
from jax import lax
from jax.experimental import pallas as pl
from jax.experimental.pallas import tpu as pltpu

F32 = jnp.float32
BF16 = jnp.bfloat16
HIGHEST = lax.Precision.HIGHEST

D = 1024
CHUNK = 128
EPS = 1e-6
HEADS = 16
HEAD_DIM = 64
GROUPS = 4
STATE = 128
N_XBC = 2048
N_DT = 2 * HEADS
N_SCAN = N_XBC + N_DT
N_REST = 5 * D
MLP_GROUPS = 8
N_EXPERTS = 16
CAPACITY_FACTOR = 2
LANES = 128
VMEM_LIMIT = 56 * 1024 * 1024


def _cparams(*sem):
    return pltpu.CompilerParams(dimension_semantics=sem, vmem_limit_bytes=VMEM_LIMIT)


def _softplus(x):
    return jnp.maximum(x, 0.0) + jnp.log1p(jnp.exp(-jnp.abs(x)))


def _rms(x, w):
    return x * lax.rsqrt(jnp.mean(x * x, axis=-1, keepdims=True) + EPS) * w


def _cast_kernel(w_ref, o_ref):
    o_ref[...] = w_ref[0].astype(o_ref.dtype)


def _to_bf16(w, layer):
    g = w.shape[1]
    return pl.pallas_call(
        _cast_kernel,
        grid=(g, 2),
        in_specs=[pl.BlockSpec((1, 1, D // 2, D), lambda i, j: (layer, i, j, 0))],
        out_specs=pl.BlockSpec((1, D // 2, D), lambda i, j: (i, j, 0)),
        out_shape=jax.ShapeDtypeStruct(w.shape[1:], BF16),
        compiler_params=_cparams("arbitrary", "arbitrary"),
        name="to_bf16",
    )(w)


def _ada_kernel(c_ref, w_ref, b_ref, o_ref):
    a = jax.nn.silu(c_ref[...]).astype(BF16)
    o_ref[0] = jnp.dot(a, w_ref[0].astype(BF16), preferred_element_type=F32) + b_ref[0]


def _ada(cc, ada_w, ada_b):
    n_layers = ada_w.shape[0]
    rows = cc.shape[0]
    return pl.pallas_call(
        _ada_kernel,
        grid=(n_layers, 6),
        in_specs=[pl.BlockSpec((rows, D), lambda l, j: (0, 0)),
                  pl.BlockSpec((1, D, D), lambda l, j: (l, 0, j)),
                  pl.BlockSpec((1, 1, D), lambda l, j: (l, 0, j))],
        out_specs=pl.BlockSpec((1, rows, D), lambda l, j: (l, 0, j)),
        out_shape=jax.ShapeDtypeStruct((n_layers, rows, 6 * D), F32),
        compiler_params=_cparams("arbitrary", "arbitrary"),
        name="ada",
    )(cc, ada_w, ada_b.reshape(n_layers, 1, 6 * D))


HALO = 8


def _in_proj_kernel(seq_len, with_res, *refs):
    if with_res:
        (hp_ref, h_ref, hn_ref, mp_ref, m_ref, mn_ref, g_ref, sc_ref, sh_ref, nw_ref, w_ref,
         wdt_ref, cw_ref, cb_ref, xbc_ref, dt_ref, hout_ref, *rest) = refs
        h = h_ref[...] + g_ref[0] * m_ref[...]
        hout_ref[...] = h
        he = (jnp.concatenate([hp_ref[...], hn_ref[...]], axis=0)
              + g_ref[0] * jnp.concatenate([mp_ref[...], mn_ref[...]], axis=0))
    else:
        (hp_ref, h_ref, hn_ref, sc_ref, sh_ref, nw_ref, w_ref,
         wdt_ref, cw_ref, cb_ref, xbc_ref, dt_ref, *rest) = refs
        h = h_ref[...]
        he = jnp.concatenate([hp_ref[...], hn_ref[...]], axis=0)
    tm = h.shape[0]
    win_ref = rest[-1]
    norm = lambda v: (_rms(v, nw_ref[...]) * (1.0 + sc_ref[0]) + sh_ref[0]).astype(BF16)
    xm = norm(h)
    xe = norm(he)
    dt_ref[...] = jnp.dot(xm, wdt_ref[...], preferred_element_type=F32)

    row = lax.broadcasted_iota(jnp.int32, (tm, D), 0)
    if seq_len >= tm:
        pos = (pl.program_id(0) % (seq_len // tm)) * tm + row
        first = pos == 0
        last = pos == seq_len - 1
    else:
        first = functools.reduce(jnp.logical_or,
                                 [row == k * seq_len for k in range(tm // seq_len)])
        last = functools.reduce(jnp.logical_or,
                                [row == (k + 1) * seq_len - 1 for k in range(tm // seq_len)])
    for j in range(N_XBC // D):
        cols = slice(j * D, (j + 1) * D)
        pm = jnp.dot(xm, w_ref[:, cols], preferred_element_type=F32)
        pe = jnp.dot(xe, w_ref[:, cols], preferred_element_type=F32)
        win = win_ref.at[j]
        win[0:HALO, :] = pe[:HALO]
        win[HALO:HALO + tm, :] = pm
        win[HALO + tm:, :] = pe[HALO:]
        prev = win[HALO - 1:HALO - 1 + tm, :]
        nxt = win[HALO + 1:HALO + 1 + tm, :]
        y = (jnp.where(first, 0.0, prev) * cw_ref[0:1, cols] + pm * cw_ref[1:2, cols]
             + jnp.where(last, 0.0, nxt) * cw_ref[2:3, cols] + cb_ref[:, cols])
        xbc_ref[:, cols] = jax.nn.silu(y).astype(BF16)
    if len(rest) > 1:
        p_ref = rest[0]
        for j in range(p_ref.shape[1] // D):
            cols = slice(N_XBC + j * D, N_XBC + (j + 1) * D)
            p_ref[:, j * D:(j + 1) * D] = jnp.dot(
                xm, w_ref[:, cols], preferred_element_type=F32).astype(BF16)


def _in_proj(h2, sc, sh, norm_w, w_main, w_dt, conv_w, conv_b, seq_len, tokens_per_mod, res=None):
    m = h2.shape[0]
    n_rest = w_main.shape[1] - N_XBC
    tm = min(512, tokens_per_mod)
    tiles_per_mod = tokens_per_mod // tm
    hb = tm // HALO
    with_res = res is not None
    row = lambda width: pl.BlockSpec((tm, width), lambda i: (i, 0))
    halo_prev = pl.BlockSpec((HALO, D), lambda i: (jnp.maximum(i * hb - 1, 0), 0))
    halo_next = pl.BlockSpec((HALO, D), lambda i: (jnp.minimum((i + 1) * hb, m // HALO - 1), 0))
    mod = pl.BlockSpec((1, 1, D), lambda i: (i // tiles_per_mod, 0, 0))
    resident = lambda shape: pl.BlockSpec(shape, lambda i: (0, 0), pipeline_mode=pl.Buffered(1))
    in_specs = [halo_prev, row(D), halo_next]
    args = [h2, h2, h2]
    if with_res:
        in_specs += [halo_prev, row(D), halo_next, mod]
        args += [res[0], res[0], res[0], res[1]]
    in_specs += [mod, mod, resident((1, D)), resident(w_main.shape), resident((D, LANES)),
                 resident((3, N_XBC)), resident((1, N_XBC))]
    args += [sc, sh, norm_w.reshape(1, D), w_main, w_dt, conv_w, conv_b.reshape(1, N_XBC)]
    out_specs = [row(N_XBC), row(LANES)]
    out_shape = [jax.ShapeDtypeStruct((m, N_XBC), BF16), jax.ShapeDtypeStruct((m, LANES), F32)]
    if with_res:
        out_specs.append(row(D))
        out_shape.append(jax.ShapeDtypeStruct((m, D), F32))
    if n_rest:
        out_specs.append(row(n_rest))
        out_shape.append(jax.ShapeDtypeStruct((m, n_rest), BF16))
    return pl.pallas_call(
        functools.partial(_in_proj_kernel, seq_len, with_res),
        grid=(m // tm,),
        in_specs=in_specs, out_specs=out_specs, out_shape=out_shape,
        scratch_shapes=[pltpu.VMEM((N_XBC // D, tm + 2 * HALO, D), F32)],
        compiler_params=_cparams("arbitrary"),
        name="in_proj_res" if with_res else "in_proj",
    )(*args)


def _dtprep_kernel(dt_ref, dtbias_ref, alog_ref, acum_ref, rows_ref):
    nc = rows_ref.shape[1]
    ri = lax.broadcasted_iota(jnp.int32, (CHUNK, CHUNK), 0)
    ci = lax.broadcasted_iota(jnp.int32, (CHUNK, CHUNK), 1)
    tri_f = (ci <= ri).astype(F32)
    tri_b = (ci >= ri).astype(F32)
    fwd_lane = lax.broadcasted_iota(jnp.int32, (CHUNK, LANES), 1) < HEADS
    neg_a = -jnp.exp(alog_ref[...])
    for c in range(nc):
        rs = slice(c * CHUNK, (c + 1) * CHUNK)
        dtv = _softplus(dt_ref[0, rs, :] + dtbias_ref[...])
        a = dtv * neg_a
        acum = jnp.where(
            fwd_lane,
            jnp.dot(tri_f, a, precision=HIGHEST, preferred_element_type=F32),
            jnp.dot(tri_b, a, precision=HIGHEST, preferred_element_type=F32))
        tot = jnp.where(fwd_lane[0:1, :], acum[CHUNK - 1:CHUNK, :], acum[0:1, :])
        w = jnp.exp(tot - acum) * dtv
        acum_ref[0, rs, :] = acum
        rows_ref[0, c, 0:N_DT, :] = acum.T[0:N_DT, :]
        rows_ref[0, c, N_DT:2 * N_DT, :] = dtv.T[0:N_DT, :]
        rows_ref[0, c, 2 * N_DT:3 * N_DT, :] = w.T[0:N_DT, :]


def _dtprep(dt3, dt_bias, a_log):
    b, l, _ = dt3.shape
    nc = l // CHUNK
    pad = lambda v: jnp.pad(v.reshape(1, N_DT), ((0, 0), (0, LANES - N_DT)))
    vec = pl.BlockSpec((1, LANES), lambda i: (0, 0))
    return pl.pallas_call(
        _dtprep_kernel,
        grid=(b,),
        in_specs=[pl.BlockSpec((1, l, LANES), lambda i: (i, 0, 0)), vec, vec],
        out_specs=[pl.BlockSpec((1, l, LANES), lambda i: (i, 0, 0)),
                   pl.BlockSpec((1, nc, 3 * N_DT, CHUNK), lambda i: (i, 0, 0, 0))],
        out_shape=[jax.ShapeDtypeStruct((b, l, LANES), F32),
                   jax.ShapeDtypeStruct((b, nc, 3 * N_DT, CHUNK), F32)],
        compiler_params=_cparams("arbitrary"),
        name="dtprep",
    )(dt3, pad(dt_bias), pad(a_log))


def _ssd_kernel(xf_ref, bf_ref, cf_ref, af_ref, rf_ref, xb_ref, bb_ref, cb_ref, ab_ref, rb_ref,
                h0f_ref, h0b_ref, yf_ref, yb_ref, stf_ref, stb_ref):
    @pl.when(pl.program_id(1) == 0)
    def _():
        stf_ref[...] = h0f_ref[...]
        stb_ref[...] = h0b_ref[...]

    ri = lax.broadcasted_iota(jnp.int32, (CHUNK, CHUNK), 0)
    ci = lax.broadcasted_iota(jnp.int32, (CHUNK, CHUNK), 1)
    lane_lo = ci < HEAD_DIM
    directions = ((xf_ref, bf_ref, cf_ref, af_ref, rf_ref, yf_ref, stf_ref),
                  (xb_ref, bb_ref, cb_ref, ab_ref, rb_ref, yb_ref, stb_ref))
    streams = [(bi, d) + refs for bi in range(xf_ref.shape[0])
               for d, refs in enumerate(directions)]
    for bi, d, x_ref, b_ref, c_ref, a_ref, r_ref, y_ref, st_ref in streams:
        mask = (ci <= ri) if d == 0 else (ci >= ri)
        acum = a_ref[bi]
        tot = acum[CHUNK - 1:CHUNK, :] if d == 0 else acum[0:1, :]
        cdec = jnp.exp(tot)
        acum_t = r_ref[bi, 0, 0:N_DT, :]
        dtv_t = r_ref[bi, 0, N_DT:2 * N_DT, :]
        w_t = r_ref[bi, 0, 2 * N_DT:3 * N_DT, :]
        for g in range(GROUPS):
            bg = b_ref[bi, :, g * STATE:(g + 1) * STATE]
            cg = c_ref[bi, :, g * STATE:(g + 1) * STATE]
            cbm = lax.dot_general(cg, bg, (((1,), (1,)), ((), ())),
                                  preferred_element_type=F32)
            bg_t = bg.astype(F32).T
            for q in range(2):
                cols = slice((2 * g + q) * LANES, (2 * g + q + 1) * LANES)
                xp = x_ref[bi, :, cols]
                zero = jnp.zeros_like(xp)
                rhs = jnp.concatenate([jnp.where(lane_lo, xp, zero),
                                       jnp.where(lane_lo, zero, xp)], axis=0)
                lane0 = d * HEADS + g * 4 + q * 2
                m_parts, s_parts, e_parts = [], [], []
                for r2 in range(2):
                    ln = lane0 + r2
                    colb = jnp.broadcast_to(acum[:, ln:ln + 1], (CHUNK, CHUNK))
                    seg = colb - acum_t[ln:ln + 1, :]
                    dec = jnp.exp(jnp.where(mask, seg, -jnp.inf))
                    m_parts.append((cbm * dec * dtv_t[ln:ln + 1, :]).astype(BF16))
                    s_parts.append((bg_t * w_t[ln:ln + 1, :]).astype(BF16))
                    e_parts.append(jnp.exp(colb))
                lhs_y = jnp.concatenate(m_parts, axis=1)
                lhs_s = jnp.concatenate(s_parts, axis=1)
                h_t = st_ref[bi, g, :, q * LANES:(q + 1) * LANES]
                y_diag = jnp.dot(lhs_y, rhs, preferred_element_type=F32)
                y_off = (jnp.dot(cg, h_t.astype(BF16), preferred_element_type=F32)
                         * jnp.where(lane_lo, e_parts[0], e_parts[1]))
                y_ref[bi, :, cols] = (y_diag + y_off).astype(y_ref.dtype)
                cd = jnp.where(lane_lo[0:1, :],
                               jnp.broadcast_to(cdec[:, lane0:lane0 + 1], (1, LANES)),
                               jnp.broadcast_to(cdec[:, lane0 + 1:lane0 + 2], (1, LANES)))
                st_ref[bi, g, :, q * LANES:(q + 1) * LANES] = (
                    h_t * cd + jnp.dot(lhs_s, rhs, preferred_element_type=F32))


def _ssd(xbc, dt3, dt_bias, a_log, h0f, h0b):
    b, l, _ = xbc.shape
    nc = l // CHUNK
    acum, rows = _dtprep(dt3, dt_bias, a_log)
    fwd = lambda k: (lambda i, c: (i, c, k))
    bwd = lambda k: (lambda i, c: (i, nc - 1 - c, k))
    nb = 2 if b % 2 == 0 else 1
    xspec = lambda f: pl.BlockSpec((nb, CHUNK, D), f(0))
    bspec = lambda f: pl.BlockSpec((nb, CHUNK, GROUPS * STATE), f(2))
    cspec = lambda f: pl.BlockSpec((nb, CHUNK, GROUPS * STATE), f(3))
    dspec = lambda f: pl.BlockSpec((nb, CHUNK, LANES), f(0))
    rspec = lambda rev: pl.BlockSpec(
        (nb, 1, 3 * N_DT, CHUNK),
        (lambda i, c: (i, nc - 1 - c, 0, 0)) if rev else (lambda i, c: (i, c, 0, 0)))
    st = pl.BlockSpec((nb, GROUPS, STATE, 4 * HEAD_DIM), lambda i, c: (i, 0, 0, 0))
    return pl.pallas_call(
        _ssd_kernel,
        grid=(b // nb, nc),
        in_specs=[xspec(fwd), bspec(fwd), cspec(fwd), dspec(fwd), rspec(False),
                  xspec(bwd), bspec(bwd), cspec(bwd), dspec(bwd), rspec(True),
                  st, st],
        out_specs=[xspec(fwd), xspec(bwd), st, st],
        out_shape=[jax.ShapeDtypeStruct((b, l, D), BF16),
                   jax.ShapeDtypeStruct((b, l, D), BF16),
                   jax.ShapeDtypeStruct(h0f.shape, F32),
                   jax.ShapeDtypeStruct(h0b.shape, F32)],
        compiler_params=_cparams("arbitrary", "arbitrary"),
        name="ssd",
    )(xbc, xbc, xbc, acum, rows, xbc, xbc, xbc, acum, rows, h0f, h0b)


def _split_bf16(x):
    hi = x.astype(BF16)
    return hi, (x - hi.astype(F32)).astype(BF16)


def _mixer_kernel(sub, yf_ref, yb_ref, xs_ref, z_ref, u_ref, v_ref, ga_ref, gb_ref, h_ref,
                  g1_ref, sc2_ref, sh2_ref, dskip_ref, snw_ref, lnw_ref, lnb_ref, n2w_ref,
                  ws_ref, bs_ref, wpa_ref, wpb_ref, wo_ref, rwh_ref, rwl_ref,
                  hout_ref, xf_ref, afft_ref):
    tm = h_ref.shape[1]
    gw = D // GROUPS
    lane = lax.broadcasted_iota(jnp.int32, (sub, LANES), 1)
    for s in range(tm // sub):
        rs = slice(s * sub, (s + 1) * sub)
        y = (yf_ref[0, rs, :].astype(F32) + yb_ref[0, rs, :].astype(F32)
             + dskip_ref[...] * xs_ref[0, rs, :].astype(F32))
        gated = y * jax.nn.silu(z_ref[0, rs, :].astype(F32))
        ssd = jnp.concatenate(
            [_rms(gated[:, k * gw:(k + 1) * gw], snw_ref[:, k * gw:(k + 1) * gw])
             for k in range(GROUPS)], axis=1)
        u = jax.nn.gelu(u_ref[0, rs, :].astype(F32))
        v = jax.nn.gelu(v_ref[0, rs, :].astype(F32))
        mu = jnp.mean(v, axis=-1, keepdims=True)
        var = jnp.mean(jnp.square(v - mu), axis=-1, keepdims=True)
        vn = ((v - mu) * lax.rsqrt(var + EPS) * lnw_ref[...] + lnb_ref[...]).astype(BF16)
        rows = []
        for ck in range(sub // CHUNK):
            cs = slice(ck * CHUNK, (ck + 1) * CHUNK)
            rows.append(jnp.concatenate(
                [jnp.dot(ws_ref[g], vn[cs, g * LANES:(g + 1) * LANES],
                         preferred_element_type=F32)
                 for g in range(MLP_GROUPS)], axis=1) + bs_ref[...])
        mixed = jnp.concatenate(rows, axis=0) if len(rows) > 1 else rows[0]
        sgu = u * mixed
        pa = jnp.dot(ssd.astype(BF16), wpa_ref[...], preferred_element_type=F32)
        pb = jnp.dot(sgu.astype(BF16), wpb_ref[...], preferred_element_type=F32)
        merged = (jax.nn.sigmoid(ga_ref[0, rs, :].astype(F32)) * pa
                  + jax.nn.sigmoid(gb_ref[0, rs, :].astype(F32)) * pb)
        out = jnp.dot(merged.astype(BF16), wo_ref[...], preferred_element_type=F32)
        hn = h_ref[0, rs, :] + g1_ref[0] * out
        hout_ref[0, rs, :] = hn
        xf = _rms(hn, n2w_ref[...]) * (1.0 + sc2_ref[0]) + sh2_ref[0]
        xf_ref[0, rs, :] = xf.astype(BF16)
        xh, xl = _split_bf16(xf)
        logits = (jnp.dot(xh, rwh_ref[...], preferred_element_type=F32)
                  + jnp.dot(xl, rwh_ref[...], preferred_element_type=F32)
                  + jnp.dot(xh, rwl_ref[...], preferred_element_type=F32))
        logits = jnp.where(lane < N_EXPERTS, logits, -jnp.inf)
        ex = jnp.exp(logits - jnp.max(logits, axis=1, keepdims=True))
        aff = ex / jnp.sum(ex, axis=1, keepdims=True)
        afft_ref[0, :, rs] = aff.T[:N_EXPERTS, :]


def _mixer(yf, yb, xbc, p3, h3, g1, sc2, sh2, consts):
    b, l, _ = h3.shape
    sub = min(256, l)
    tm = min(512, l)
    tok = lambda k: pl.BlockSpec((1, tm, D), lambda i, t: (i, t, k))
    mod = pl.BlockSpec((1, 1, D), (lambda i, t: (i, 0, 0)) if g1.shape[0] > 1
                       else (lambda i, t: (0, 0, 0)))
    const = lambda shape: pl.BlockSpec(shape, lambda i, t: (0,) * len(shape),
                                       pipeline_mode=pl.Buffered(1))
    vec = const((1, D))
    wmat = const((D, D))
    return pl.pallas_call(
        functools.partial(_mixer_kernel, sub),
        grid=(b, l // tm),
        in_specs=[tok(0), tok(0), tok(0), tok(0), tok(1), tok(2), tok(3), tok(4), tok(0),
                  mod, mod, mod, vec, vec, vec, vec, vec,
                  const((MLP_GROUPS, CHUNK, CHUNK)), const((CHUNK, D)),
                  wmat, wmat, wmat, const((D, LANES)), const((D, LANES))],
        out_specs=[tok(0), tok(0),
                   pl.BlockSpec((1, N_EXPERTS, tm), lambda i, t: (i, 0, t))],
        out_shape=[jax.ShapeDtypeStruct((b, l, D), F32),
                   jax.ShapeDtypeStruct((b, l, D), BF16),
                   jax.ShapeDtypeStruct((b, N_EXPERTS, l), F32)],
        compiler_params=_cparams("arbitrary", "arbitrary"),
        name="mixer",
    )(yf, yb, xbc, p3, p3, p3, p3, p3, h3, g1, sc2, sh2, *consts)


def _prefix_excl(m):
    n = m.shape[1]
    tri = (lax.broadcasted_iota(jnp.int32, (LANES, LANES), 0)
           <= lax.broadcasted_iota(jnp.int32, (LANES, LANES), 1)).astype(BF16)
    outs = []
    off = jnp.zeros((m.shape[0], 1), F32)
    for k in range(n // LANES):
        mk = m[:, k * LANES:(k + 1) * LANES]
        inc = jnp.dot(mk.astype(BF16), tri, preferred_element_type=F32)
        outs.append(inc - mk + off)
        off = off + inc[:, LANES - 1:LANES]
    return jnp.concatenate(outs, axis=1) if len(outs) > 1 else outs[0]


def _topk_kernel(cap, aff_ref, pos_ref):
    a = aff_ref[...]
    bits = lax.bitcast_convert_type(a, jnp.int32)
    n_exp = a.shape[0]
    count = lambda pred: jnp.sum(jnp.where(pred, 1.0, 0.0), axis=1, keepdims=True)

    def body(_, carry):
        lo, hi = carry
        mid = lo + ((hi - lo + 1) >> 1)
        ok = count(bits >= mid) >= cap
        return jnp.where(ok, mid, lo), jnp.where(ok, hi, mid - 1)

    lo0 = jnp.zeros((n_exp, 1), jnp.int32)
    hi0 = jnp.full((n_exp, 1), 0x7F800000, jnp.int32)
    thr, _ = lax.fori_loop(0, 31, body, (lo0, hi0))
    gt = bits > thr
    eq = bits == thr
    need = cap - count(gt)
    eq_rank = _prefix_excl(jnp.where(eq, 1.0, 0.0))
    sel = gt | (eq & (eq_rank < need))
    slot = _prefix_excl(jnp.where(sel, 1.0, 0.0))
    pos_ref[...] = jnp.where(sel, slot, -1.0).astype(jnp.int32)


def _topk(aff_t):
    b, n_exp, n = aff_t.shape
    cap = CAPACITY_FACTOR * n // N_EXPERTS
    rows = n_exp * min(4, b)
    spec = pl.BlockSpec((rows, n), lambda i: (i, 0))
    pos = pl.pallas_call(
        functools.partial(_topk_kernel, cap),
        grid=(b * n_exp // rows,),
        in_specs=[spec], out_specs=spec,
        out_shape=jax.ShapeDtypeStruct((b * n_exp, n), jnp.int32),
        compiler_params=_cparams("arbitrary"),
        name="topk",
    )(aff_t.reshape(b * n_exp, n))
    return pos.reshape(b, n_exp, n)


EXPERTS_PER_STEP = 2


def _moe_kernel(cap, xf_ref, aff_ref, pos_ref, wg_ref, wu_ref, wd_ref, o_ref):
    bg, n, _ = xf_ref.shape
    n_local = wg_ref.shape[0]
    step = pl.program_id(1)

    @pl.when(step == 0)
    def _():
        o_ref[...] = jnp.zeros_like(o_ref)

    slot = lax.broadcasted_iota(jnp.int32, (cap, n), 0)
    onehots = [[] for _ in range(bg)]
    ys = [[] for _ in range(bg)]
    for j in range(n_local):
        e = step * n_local + j
        gates, parts = [], []
        for k in range(bg):
            hit = pos_ref[k, pl.ds(e, 1), :] == slot
            onehot = hit.astype(BF16)
            onehots[k].append(onehot)
            parts.append(jnp.dot(onehot, xf_ref[k], preferred_element_type=F32).astype(BF16))
            gates.append(jnp.sum(jnp.where(hit, aff_ref[k, pl.ds(e, 1), :], 0.0),
                                 axis=1, keepdims=True))
        xg = jnp.concatenate(parts, axis=0) if bg > 1 else parts[0]
        hid = (jax.nn.silu(jnp.dot(xg, wg_ref[j], preferred_element_type=F32))
               * jnp.dot(xg, wu_ref[j], preferred_element_type=F32))
        y = jnp.dot(hid.astype(BF16), wd_ref[j], preferred_element_type=F32)
        for k in range(bg):
            ys[k].append((y[k * cap:(k + 1) * cap] * gates[k]).astype(BF16))
    for k in range(bg):
        o_ref[k] += lax.dot_general(jnp.concatenate(onehots[k], axis=0),
                                    jnp.concatenate(ys[k], axis=0),
                                    (((0,), (0,)), ((), ())), preferred_element_type=F32)


def _moe(xf, aff_t, pos, wg, wu, wd, bg):
    b, n, _ = xf.shape
    cap = CAPACITY_FACTOR * n // N_EXPERTS
    tokens = pl.BlockSpec((bg, n, D), lambda i, e: (i, 0, 0))
    routing = pl.BlockSpec((bg, N_EXPERTS, n), lambda i, e: (i, 0, 0))
    wspec = pl.BlockSpec((EXPERTS_PER_STEP, D, D), lambda i, e: (e, 0, 0))
    return pl.pallas_call(
        functools.partial(_moe_kernel, cap),
        grid=(b // bg, N_EXPERTS // EXPERTS_PER_STEP),
        in_specs=[tokens, routing, routing, wspec, wspec, wspec],
        out_specs=tokens,
        out_shape=jax.ShapeDtypeStruct((b, n, D), F32),
        compiler_params=_cparams("arbitrary", "arbitrary"),
        name="moe",
    )(xf, aff_t, pos, wg, wu, wd)


def _final_kernel(h_ref, moe_ref, g_ref, w_ref, o_ref):
    o_ref[0] = _rms(h_ref[0] + g_ref[0] * moe_ref[0], w_ref[...])


def _final(h3, moe3, g2, w):
    b, l, _ = h3.shape
    tm = min(512, l)
    tok = pl.BlockSpec((1, tm, D), lambda i, t: (i, t, 0))
    return pl.pallas_call(
        _final_kernel,
        grid=(b, l // tm),
        in_specs=[tok, tok, pl.BlockSpec((1, 1, D), lambda i, t: (i, 0, 0)),
                  pl.BlockSpec((1, D), lambda i, t: (0, 0))],
        out_specs=tok,
        out_shape=jax.ShapeDtypeStruct(h3.shape, F32),
        compiler_params=_cparams("arbitrary", "arbitrary"),
        name="final",
    )(h3, moe3, g2, w.reshape(1, D))


def kernel(x, c, ctx, c_ctx, ada_w, ada_b, norm1_w, norm2_w, w_in, conv_w, conv_b, dt_bias, a_log,
           d_skip, ssd_norm_w, sgu_ln_w, sgu_ln_b, w_s, b_s, w_pa, w_pb, w_o, router_w,
           w_gate, w_up, w_down, final_norm_w):
    bsz, n_lat, _ = x.shape
    n_ctx = ctx.shape[1]
    depth = ada_w.shape[0]

    mod_rows = ((bsz + 1 + 7) // 8) * 8
    cc = jnp.zeros((mod_rows, D), F32).at[:bsz].set(c).at[bsz].set(c_ctx)
    mods = _ada(cc, ada_w, ada_b)

    zero_state = jnp.zeros((bsz, GROUPS, STATE, 4 * HEAD_DIM), F32)
    h = x.reshape(bsz * n_lat, D)
    hc = ctx.reshape(bsz * n_ctx, D)
    pending_l = None
    pending_c = None
    for layer in range(depth):
        last = layer == depth - 1
        ml = [mods[layer, :bsz, k * D:(k + 1) * D].reshape(bsz, 1, D) for k in range(6)]
        mc = [mods[layer, bsz:bsz + 1, k * D:(k + 1) * D].reshape(1, 1, D) for k in range(6)]
        sh1_l, sc1_l, g1_l, sh2_l, sc2_l, g2_l = ml
        sh1_c, sc1_c, g1_c, sh2_c, sc2_c, g2_c = mc

        w_l = w_in[layer]
        w_main = jnp.concatenate([w_l[:, :N_XBC], w_l[:, N_SCAN:]], axis=1).astype(BF16)
        w_dt = jnp.pad(w_l[:, N_XBC:N_SCAN], ((0, 0), (0, LANES - N_DT))).astype(BF16)
        consts = (jnp.repeat(d_skip[layer], HEAD_DIM).reshape(1, D),
                  ssd_norm_w[layer].reshape(1, D),
                  sgu_ln_w[layer].reshape(1, D), sgu_ln_b[layer].reshape(1, D),
                  norm2_w[layer].reshape(1, D),
                  w_s[layer].astype(BF16),
                  jnp.repeat(b_s[layer].T, LANES, axis=1),
                  w_pa[layer].astype(BF16), w_pb[layer].astype(BF16), w_o[layer].astype(BF16),
                  *_split_bf16(jnp.pad(router_w[layer], ((0, 0), (0, LANES - N_EXPERTS)))))
        wg = _to_bf16(w_gate, layer)
        wu = _to_bf16(w_up, layer)
        wd = _to_bf16(w_down, layer)
        proj = functools.partial(_in_proj, norm_w=norm1_w[layer], w_dt=w_dt,
                                 conv_w=conv_w[layer], conv_b=conv_b[layer])
        scan = functools.partial(_ssd, dt_bias=dt_bias[layer], a_log=a_log[layer])

        out_c = list(proj(hc, sc1_c, sh1_c, w_main=w_main[:, :N_XBC] if last else w_main,
                          seq_len=n_ctx, tokens_per_mod=bsz * n_ctx, res=pending_c))
        if pending_c is not None:
            hc = out_c.pop(2)
        xbc_c = out_c[0].reshape(bsz, n_ctx, N_XBC)
        yf_c, yb_c, st_f, st_b = scan(xbc_c, out_c[1].reshape(bsz, n_ctx, LANES),
                                      h0f=zero_state, h0b=zero_state)

        out_l = list(proj(h, sc1_l, sh1_l, w_main=w_main, seq_len=n_lat, tokens_per_mod=n_lat,
                          res=pending_l))
        if pending_l is not None:
            h = out_l.pop(2)
        xbc_l = out_l[0].reshape(bsz, n_lat, N_XBC)
        yf_l, yb_l, _, _ = scan(xbc_l, out_l[1].reshape(bsz, n_lat, LANES), h0f=st_f, h0b=st_b)
        h3, xf_l, aff_l = _mixer(yf_l, yb_l, xbc_l, out_l[2].reshape(bsz, n_lat, N_REST),
                                 h.reshape(bsz, n_lat, D), g1_l, sc2_l, sh2_l, consts)
        moe_l = _moe(xf_l, aff_l, _topk(aff_l), wg, wu, wd, bg=1)
        h = h3.reshape(bsz * n_lat, D)
        pending_l = (moe_l.reshape(bsz * n_lat, D), g2_l)

        if not last:
            hc3, xf_c, aff_c = _mixer(yf_c, yb_c, xbc_c, out_c[2].reshape(bsz, n_ctx, N_REST),
                                      hc.reshape(bsz, n_ctx, D), g1_c, sc2_c, sh2_c, consts)
            moe_c = _moe(xf_c, aff_c, _topk(aff_c), wg, wu, wd, bg=min(8, bsz))
            hc = hc3.reshape(bsz * n_ctx, D)
            pending_c = (moe_c.reshape(bsz * n_ctx, D), g2_c)

    moe_l, g2_l = pending_l
    return _final(h.reshape(bsz, n_lat, D), moe_l.reshape(bsz, n_lat, D), g2_l, final_norm_w)
```
